```python
import math
import jax, jax.numpy as jnp
from jax import lax
import numpy as np

D_MODEL = 1024
BATCH = 32
SEQ = 256
DEPTH = 2
DEC_BATCH = 8
DEC_SEQ = 4096
PAST_LEN = 512

GRID_W = 64
F_GROUPS = 4
F_GROUP_DIM = 64
F_WIDTH = F_GROUPS * F_GROUP_DIM
MLA_HEADS = 8
QK_NOPE = 64
QK_ROPE = 32
V_DIM = 64
Q_LORA = 256
KV_LORA = 128
MLA_WIDTH = MLA_HEADS * V_DIM
GLA_HEADS = 4
GLA_DK = 32
GLA_DV = 64
GLA_WIDTH = GLA_HEADS * GLA_DV
GLA_GATE_RANK = 16
GLA_TAU = 16.0
GLA_CHUNK = 64
D_FF = 4 * D_MODEL
N_BRANCH = 3
ROPE_BASE = 10000.0
NORM_EPS = 1e-6
Q_BLOCK = 128
IN_SIZES = (F_WIDTH, Q_LORA, KV_LORA, QK_ROPE, GLA_HEADS * GLA_DK, GLA_HEADS * GLA_DK,
            GLA_WIDTH, GLA_WIDTH, GLA_GATE_RANK, GLA_GATE_RANK, N_BRANCH * D_MODEL)
D_IN = F_WIDTH + Q_LORA + KV_LORA + QK_ROPE + 2 * GLA_HEADS * GLA_DK + 2 * GLA_WIDTH + 2 * GLA_GATE_RANK + N_BRANCH * D_MODEL

kernel_name = 'hybrid_fourier_mla_gla_prefix_dit_step'


def rms_norm(x, g=None):
    xf = x.astype(jnp.float32)
    y = xf * lax.rsqrt(jnp.mean(xf * xf, axis=-1, keepdims=True) + NORM_EPS)
    if g is not None:
        y = y * g.astype(jnp.float32)
    return y.astype(x.dtype)


def split_cols(z, sizes):
    idx = [int(v) for v in np.cumsum(np.array(sizes))[:-1]]
    return jnp.split(z, idx, axis=-1)


def axial_rope(x):
    L = x.shape[1]
    rows = L // GRID_W
    row = jnp.repeat(jnp.arange(rows), GRID_W)
    col = jnp.tile(jnp.arange(GRID_W), rows)
    half = QK_ROPE // 2
    inv = ROPE_BASE ** (-jnp.arange(0, half, 2, dtype=jnp.float32) / half)

    def rot(xh, pos):
        ang = pos.astype(jnp.float32)[:, None] * inv[None, :]
        cos = jnp.cos(ang)[None, :, None, :]
        sin = jnp.sin(ang)[None, :, None, :]
        a, b = jnp.split(xh.astype(jnp.float32), 2, axis=-1)
        return jnp.concatenate([a * cos - b * sin, a * sin + b * cos], axis=-1)

    out = jnp.concatenate([rot(x[..., :half], row), rot(x[..., half:], col)], axis=-1)
    return out.astype(x.dtype)


def attend(q, k, v):
    B, L, H, dq = q.shape
    dv = v.shape[-1]
    blk = math.gcd(L, Q_BLOCK)
    n = L // blk
    qb = q.reshape(B, n, blk, H, dq).transpose(1, 0, 2, 3, 4)
    scale = dq ** -0.5

    def one(qi):
        s = jnp.einsum('bqhd,bkhd->bhqk', qi, k, preferred_element_type=jnp.float32) * scale
        p = jax.nn.softmax(s, axis=-1)
        return jnp.einsum('bhqk,bkhe->bqhe', p.astype(v.dtype), v)

    o = lax.map(one, qb)
    return o.transpose(1, 0, 2, 3, 4).reshape(B, L, H * dv)


def gla_chunked(q, k, v, log_a, s0):
    B, L, H, DK = q.shape
    DV = v.shape[-1]
    C = math.gcd(L, GLA_CHUNK)
    n = L // C

    def to_chunks(t):
        return t.reshape(B, n, C, H, t.shape[-1]).transpose(1, 0, 3, 2, 4)

    causal = jnp.tril(jnp.ones((C, C), dtype=bool))[None, None, :, :, None]

    def step(S, inp):
        qi, ki, vi, gi = inp
        b = jnp.cumsum(gi, axis=2)
        inter = jnp.einsum('bhid,bhde->bhie', qi * jnp.exp(b), S)
        diff = b[:, :, :, None, :] - b[:, :, None, :, :]
        decay = jnp.where(causal, jnp.exp(jnp.minimum(diff, 0.0)), 0.0)
        scores = jnp.einsum('bhid,bhjd,bhijd->bhij', qi, ki, decay)
        intra = jnp.einsum('bhij,bhje->bhie', scores, vi)
        b_last = b[:, :, -1:, :]
        S_new = jnp.exp(b_last[:, :, 0, :])[..., None] * S + jnp.einsum(
            'bhjd,bhje->bhde', ki * jnp.exp(b_last - b), vi)
        return S_new, inter + intra

    S_fin, o = lax.scan(step, s0, (to_chunks(q), to_chunks(k), to_chunks(v), to_chunks(log_a)))
    o = o.transpose(1, 0, 3, 2, 4).reshape(B, L, H, DV)
    return o, S_fin


def mixer(h, p, ctx):
    B, L, _ = h.shape
    z = h @ p['w_in']
    zf, zqd, zkvd, zkr, zgq, zgk, zgv, zgr, zaf, zab, zgate = split_cols(z, IN_SIZES)

    f = zf.reshape(B, L, F_GROUPS, F_GROUP_DIM).astype(jnp.float32)
    f = jnp.real(jnp.fft.fft2(f, axes=(1, 3), norm='ortho')).reshape(B, L, F_WIDTH).astype(h.dtype)
    y_a = f @ p['w_o_fourier']

    cq = rms_norm(zqd, p['mla_q_norm_g'])
    q = (cq @ p['w_q_up']).reshape(B, L, MLA_HEADS, QK_NOPE + QK_ROPE)
    ckv = rms_norm(zkvd, p['mla_kv_norm_g'])
    k_rope = zkr
    if ctx is None:
        q_att = q
        ckv_all, kr_all = ckv, k_rope
    else:
        q_att = jnp.concatenate([q[..., :QK_NOPE], axial_rope(q[..., QK_NOPE:])], axis=-1)
        kr_lat = axial_rope(k_rope[:, :, None, :])[:, :, 0, :]
        ckv_all = jnp.concatenate([ctx[0].astype(ckv.dtype), ckv], axis=1)
        kr_all = jnp.concatenate([ctx[1].astype(kr_lat.dtype), kr_lat], axis=1)
    Lk = ckv_all.shape[1]
    kv = (ckv_all @ p['w_kv_up']).reshape(B, Lk, MLA_HEADS, QK_NOPE + V_DIM)
    k_nope, v = kv[..., :QK_NOPE], kv[..., QK_NOPE:]
    k = jnp.concatenate(
        [k_nope, jnp.broadcast_to(kr_all[:, :, None, :], (B, Lk, MLA_HEADS, QK_ROPE))], axis=-1)
    y_b = attend(q_att, k, v) @ p['w_o_mla']

    f32 = jnp.float32
    gq = zgq.reshape(B, L, GLA_HEADS, GLA_DK).astype(f32) * (GLA_DK ** -0.5)
    gk = zgk.reshape(B, L, GLA_HEADS, GLA_DK).astype(f32)
    gv = zgv.reshape(B, L, GLA_HEADS, GLA_DV).astype(f32)
    la_f = jax.nn.log_sigmoid((zaf @ p['gla_wa2_f'] + p['gla_ba_f']).astype(f32)) / GLA_TAU
    la_b = jax.nn.log_sigmoid((zab @ p['gla_wa2_b'] + p['gla_ba_b']).astype(f32)) / GLA_TAU
    la_f = la_f.reshape(B, L, GLA_HEADS, GLA_DK)
    la_b = la_b.reshape(B, L, GLA_HEADS, GLA_DK)
    if ctx is None:
        s0_f = jnp.zeros((B, GLA_HEADS, GLA_DK, GLA_DV), f32)
        s0_b = jnp.zeros((B, GLA_HEADS, GLA_DK, GLA_DV), f32)
    else:
        s0_f = ctx[2].astype(f32)
        s0_b = ctx[3].astype(f32)
    o_f, s_f = gla_chunked(gq, gk, gv, la_f, s0_f)
    o_b, s_b = gla_chunked(jnp.flip(gq, 1), jnp.flip(gk, 1), jnp.flip(gv, 1), jnp.flip(la_b, 1), s0_b)
    o_g = rms_norm(o_f + jnp.flip(o_b, 1)).reshape(B, L, GLA_WIDTH) * p['gla_norm_g'].astype(f32)
    o_g = (o_g * jax.nn.silu(zgr.astype(f32))).astype(h.dtype)
    y_c = o_g @ p['w_o_gla']

    g = jax.nn.sigmoid(zgate.astype(f32)).astype(h.dtype).reshape(B, L, N_BRANCH, D_MODEL)
    merged = g[:, :, 0] * y_a + g[:, :, 1] * y_b + g[:, :, 2] * y_c
    out = merged @ p['w_out']
    return out, (ckv, k_rope, s_f.astype(h.dtype), s_b.astype(h.dtype))


def layer(x, cond, p, ctx):
    mod = (jax.nn.silu(cond) @ p['w_mod'] + p['b_mod'])[:, None, :]
    shift1, scale1, gate1, shift2, scale2, gate2 = jnp.split(mod, 6, axis=-1)
    h = rms_norm(x, p['norm1_g']) * (1 + scale1) + shift1
    out, ctx_out = mixer(h, p, ctx)
    x = x + gate1 * out
    h2 = rms_norm(x, p['norm2_g']) * (1 + scale2) + shift2
    x = x + gate2 * (jnp.square(jax.nn.relu(h2 @ p['w_ff1'])) @ p['w_ff2'])
    return x, ctx_out


def setup_inputs(seed: int = 0) -> dict:
    key = jax.random.key(seed)
    ks = jax.random.split(key, 32)

    def nrm(k, shape, s):
        return jax.random.normal(k, shape, jnp.float32) * s

    return {
        'x_prompt': nrm(ks[0], (BATCH, SEQ, D_MODEL), 1.0),
        'x_sample': nrm(ks[1], (DEC_BATCH, DEC_SEQ, D_MODEL), 1.0),
        'c': nrm(ks[2], (DEC_BATCH, D_MODEL), 1.0),
        'cache_mla_ckv': nrm(ks[3], (DEC_BATCH, DEPTH, PAST_LEN, KV_LORA), 1.0),
        'cache_mla_krope': nrm(ks[4], (DEC_BATCH, DEPTH, PAST_LEN, QK_ROPE), 1.0),
        'state_gla': nrm(ks[5], (DEC_BATCH, DEPTH, 2, GLA_HEADS, GLA_DK, GLA_DV), 0.3),
        'c_ctx': nrm(ks[6], (D_MODEL,), 1.0),
        'w_mod': nrm(ks[7], (DEPTH, D_MODEL, 6 * D_MODEL), D_MODEL ** -0.5),
        'b_mod': nrm(ks[8], (DEPTH, 6 * D_MODEL), 0.02),
        'norm1_g': 1.0 + nrm(ks[9], (DEPTH, D_MODEL), 0.05),
        'w_in': nrm(ks[10], (DEPTH, D_MODEL, D_IN), D_MODEL ** -0.5),
        'mla_q_norm_g': 1.0 + nrm(ks[11], (DEPTH, Q_LORA), 0.05),
        'mla_kv_norm_g': 1.0 + nrm(ks[12], (DEPTH, KV_LORA), 0.05),
        'w_q_up': nrm(ks[13], (DEPTH, Q_LORA, MLA_HEADS * (QK_NOPE + QK_ROPE)), Q_LORA ** -0.5),
        'w_kv_up': nrm(ks[14], (DEPTH, KV_LORA, MLA_HEADS * (QK_NOPE + V_DIM)), KV_LORA ** -0.5),
        'gla_wa2_f': nrm(ks[15], (DEPTH, GLA_GATE_RANK, GLA_HEADS * GLA_DK), GLA_GATE_RANK ** -0.5),
        'gla_ba_f': nrm(ks[16], (DEPTH, GLA_HEADS * GLA_DK), 0.5),
        'gla_wa2_b': nrm(ks[17], (DEPTH, GLA_GATE_RANK, GLA_HEADS * GLA_DK), GLA_GATE_RANK ** -0.5),
        'gla_ba_b': nrm(ks[18], (DEPTH, GLA_HEADS * GLA_DK), 0.5),
        'gla_norm_g': 1.0 + nrm(ks[19], (DEPTH, GLA_WIDTH), 0.05),
        'w_o_fourier': nrm(ks[20], (DEPTH, F_WIDTH, D_MODEL), F_WIDTH ** -0.5),
        'w_o_mla': nrm(ks[21], (DEPTH, MLA_WIDTH, D_MODEL), MLA_WIDTH ** -0.5),
        'w_o_gla': nrm(ks[22], (DEPTH, GLA_WIDTH, D_MODEL), GLA_WIDTH ** -0.5),
        'w_out': nrm(ks[23], (DEPTH, D_MODEL, D_MODEL), D_MODEL ** -0.5),
        'norm2_g': 1.0 + nrm(ks[24], (DEPTH, D_MODEL), 0.05),
        'w_ff1': nrm(ks[25], (DEPTH, D_MODEL, D_FF), D_MODEL ** -0.5),
        'w_ff2': nrm(ks[26], (DEPTH, D_FF, D_MODEL), D_FF ** -0.5),
        'final_norm_g': 1.0 + nrm(ks[27], (D_MODEL,), 0.05),
    }


def reference(x_prompt, x_sample, c, cache_mla_ckv, cache_mla_krope, state_gla, c_ctx,
              w_mod, b_mod, norm1_g, w_in, mla_q_norm_g, mla_kv_norm_g, w_q_up, w_kv_up,
              gla_wa2_f, gla_ba_f, gla_wa2_b, gla_ba_b, gla_norm_g, w_o_fourier, w_o_mla,
              w_o_gla, w_out, norm2_g, w_ff1, w_ff2, final_norm_g):
    xp = x_prompt
    xs = x_sample
    cond_ctx = c_ctx[None, :]
    ckv_list, kr_list, gla_list = [], [], []
    for l in range(DEPTH):
        p = dict(w_mod=w_mod[l], b_mod=b_mod[l], norm1_g=norm1_g[l], w_in=w_in[l],
                 mla_q_norm_g=mla_q_norm_g[l], mla_kv_norm_g=mla_kv_norm_g[l],
                 w_q_up=w_q_up[l], w_kv_up=w_kv_up[l], gla_wa2_f=gla_wa2_f[l],
                 gla_ba_f=gla_ba_f[l], gla_wa2_b=gla_wa2_b[l], gla_ba_b=gla_ba_b[l],
                 gla_norm_g=gla_norm_g[l], w_o_fourier=w_o_fourier[l], w_o_mla=w_o_mla[l],
                 w_o_gla=w_o_gla[l], w_out=w_out[l], norm2_g=norm2_g[l], w_ff1=w_ff1[l],
                 w_ff2=w_ff2[l])
        xp, (ckv_c, kr_c, sf_c, sb_c) = layer(xp, cond_ctx, p, None)
        ckv_list.append(ckv_c)
        kr_list.append(kr_c)
        gla_list.append(jnp.stack([sf_c, sb_c], axis=1))
        ctx = (cache_mla_ckv[:, l], cache_mla_krope[:, l], state_gla[:, l, 0], state_gla[:, l, 1])
        xs, _ = layer(xs, c, p, ctx)
    y_prompt = rms_norm(xp, final_norm_g)
    y_sample = rms_norm(xs, final_norm_g)
    new_mla_ckv = jnp.stack(ckv_list, axis=1)
    new_mla_krope = jnp.stack(kr_list, axis=1)
    new_state_gla = jnp.stack(gla_list, axis=1)
    return (y_prompt, y_sample, new_mla_ckv, new_mla_krope, new_state_gla)
```

```python
import functools
import math

import numpy as np
import jax
import jax.numpy as jnp
from jax import lax
from jax.experimental import pallas as pl
from jax.experimental.pallas import tpu as pltpu

F32 = jnp.float32
BF16 = jnp.bfloat16

D_MODEL = 1024
GRID_W = 64
F_GROUPS = 4
F_GROUP_DIM = 64
F_WIDTH = F_GROUPS * F_GROUP_DIM
MLA_HEADS = 8
QK_NOPE = 64
QK_ROPE = 32
V_DIM = 64
Q_LORA = 256
KV_LORA = 128
MLA_WIDTH = MLA_HEADS * V_DIM
GLA_HEADS = 4
GLA_DK = 32
GLA_DV = 64
GLA_WIDTH = GLA_HEADS * GLA_DV
GLA_QK = GLA_HEADS * GLA_DK
GLA_GATE_RANK = 16
GLA_TAU = 16.0
D_FF = 4 * D_MODEL
N_BRANCH = 3
ROPE_BASE = 10000.0
NORM_EPS = 1e-6
N_MOD = 6

LANES = 128
VMEM_LIMIT_BYTES = 56 * 1024 * 1024

HEAD_PAD = LANES
HEADS_PER_STEP = 2
GLA_SUB = 16
GLA_TILE = 128
MOD_ROWS = 16

C_F = 0
C_QD = C_F + F_WIDTH
C_KVD = C_QD + Q_LORA
C_GQ = C_KVD + KV_LORA
C_GK = C_GQ + GLA_QK
C_GV = C_GK + GLA_QK
C_GR = C_GV + GLA_WIDTH
C_GATE = C_GR + GLA_WIDTH
C_SMALL = C_GATE + N_BRANCH * D_MODEL
W1_COLS = C_SMALL + LANES
S_KR = 0
S_AF = 64
S_AB = 80


def _dot(a, b):
    return jnp.dot(a, b, preferred_element_type=F32)


def _dot_nt(a, b):
    return lax.dot_general(a, b, (((1,), (1,)), ((), ())), preferred_element_type=F32)


def _dot_tn(a, b):
    return lax.dot_general(a, b, (((0,), (0,)), ((), ())), preferred_element_type=F32)


def _split3(x):
    hi = x.astype(BF16)
    r1 = x - hi.astype(F32)
    mid = r1.astype(BF16)
    lo = (r1 - mid.astype(F32)).astype(BF16)
    return hi, mid, lo


def _dot_exact_lhs(m, x):
    hi, mid, lo = _split3(x)
    return _dot(m, hi) + _dot(m, mid) + _dot(m, lo)


def _rms(x):
    return x * lax.rsqrt(jnp.mean(x * x, axis=-1, keepdims=True) + NORM_EPS)


def _sigmoid(x):
    return 1.0 / (1.0 + jnp.exp(-x))


def _const_spec(shape):
    return pl.BlockSpec(shape, lambda *_: (0,) * len(shape))


def _params(sem):
    return pltpu.CompilerParams(dimension_semantics=sem, vmem_limit_bytes=VMEM_LIMIT_BYTES)


def _mod_kernel(c_ref, w_ref, b_ref, o_ref):
    c = c_ref[...]
    s = (c * _sigmoid(c)).astype(BF16)
    o_ref[0] = _dot(s, w_ref[0].astype(BF16)) + b_ref[0]


def _mod_call(cond, w_mod, b_mod):
    depth = w_mod.shape[0]
    n_col = w_mod.shape[2] // D_MODEL
    return pl.pallas_call(
        _mod_kernel,
        grid=(depth, n_col),
        in_specs=[
            pl.BlockSpec((MOD_ROWS, D_MODEL), lambda l, j: (0, 0)),
            pl.BlockSpec((1, D_MODEL, D_MODEL), lambda l, j: (l, 0, j)),
            pl.BlockSpec((1, 1, D_MODEL), lambda l, j: (l, 0, j)),
        ],
        out_specs=pl.BlockSpec((1, MOD_ROWS, D_MODEL), lambda l, j: (l, 0, j)),
        out_shape=jax.ShapeDtypeStruct((depth, MOD_ROWS, w_mod.shape[2]), F32),
        compiler_params=_params(("arbitrary", "arbitrary")),
        name="mod_proj",
    )(cond, w_mod, b_mod.reshape(depth, 1, -1))


def _inproj_kernel(*refs, rope, emit_ctx):
    it = iter(refs)
    x_ref, mod_ref, g1_ref, w1_ref, dft_ref, gq_ref, wq_ref = (next(it) for _ in range(7))
    wqr_ref = next(it) if rope else None
    gkv_ref, wkt_ref, wv_ref, wkrt_ref, wa_ref, ba_ref = (next(it) for _ in range(6))
    if rope:
        cq_ref, sq_ref, ct_ref, st_ref = (next(it) for _ in range(4))
    fcs_ref, q_ref, kt_ref, v_ref, qk_ref, gv_ref, la_ref, sg_ref, gate_ref = (next(it) for _ in range(9))
    if emit_ctx:
        ckv_ref, kr_ref = next(it), next(it)

    x = x_ref[0]
    mod = mod_ref[0]
    shift1, scale1 = mod[0:1], mod[1:2]
    h = _rms(x) * g1_ref[...] * (1.0 + scale1) + shift1
    hb = h.astype(BF16)

    zf = _dot(hb, w1_ref[:, C_F:C_F + F_WIDTH])
    fcs_ref[0] = _dot(zf.astype(BF16), dft_ref[...]).astype(BF16)

    cq = (_rms(_dot(hb, w1_ref[:, C_QD:C_QD + Q_LORA])) * gq_ref[...]).astype(BF16)
    qa = _dot(cq, wq_ref[...])
    scale = (QK_NOPE + QK_ROPE) ** -0.5
    if rope:
        qb = _dot(cq, wqr_ref[...])
        cos_q, sin_q = cq_ref[...], sq_ref[...]
        for hd in range(MLA_HEADS):
            sl = slice(hd * HEAD_PAD, (hd + 1) * HEAD_PAD)
            q_ref[0, :, sl] = ((qa[:, sl] * cos_q + qb[:, sl] * sin_q) * scale).astype(BF16)
    else:
        q_ref[0] = (qa * scale).astype(BF16)

    ckv = _rms(_dot(hb, w1_ref[:, C_KVD:C_KVD + KV_LORA])) * gkv_ref[...]
    ckvb = ckv.astype(BF16)
    knt = _dot_nt(wkt_ref[...], ckvb)
    v_ref[0] = _dot(ckvb, wv_ref[...]).astype(BF16)
    krt2 = _dot_nt(wkrt_ref[...], hb)
    krt = krt2[0:QK_ROPE]
    if rope:
        krt = krt * ct_ref[...] + krt2[QK_ROPE:2 * QK_ROPE] * st_ref[...]
    krt = krt.astype(BF16)
    zero_pad = jnp.zeros((HEAD_PAD - QK_NOPE - QK_ROPE, krt.shape[1]), BF16)
    for hd in range(MLA_HEADS):
        kt_ref[0, hd, 0:QK_NOPE, :] = knt[hd * QK_NOPE:(hd + 1) * QK_NOPE].astype(BF16)
        kt_ref[0, hd, QK_NOPE:QK_NOPE + QK_ROPE, :] = krt
        kt_ref[0, hd, QK_NOPE + QK_ROPE:HEAD_PAD, :] = zero_pad

    small = _dot(hb, w1_ref[:, C_SMALL:C_SMALL + LANES])
    if emit_ctx:
        ckv_ref[0] = ckv
        kr_ref[0] = small[:, S_KR:S_KR + QK_ROPE]

    gq = _dot(hb, w1_ref[:, C_GQ:C_GQ + GLA_QK]) * (GLA_DK ** -0.5)
    gk = _dot(hb, w1_ref[:, C_GK:C_GK + GLA_QK])
    qk_ref[0, :, 0:GLA_QK] = gq
    qk_ref[0, :, GLA_QK:2 * GLA_QK] = gk
    gv_ref[0] = _dot(hb, w1_ref[:, C_GV:C_GV + GLA_WIDTH])
    a_pre = _dot(small.astype(BF16), wa_ref[...]) + ba_ref[...]
    log_sig = jnp.minimum(a_pre, 0.0) - jnp.log(1.0 + jnp.exp(-jnp.abs(a_pre)))
    la_ref[0] = log_sig * (1.0 / GLA_TAU)
    zgr = _dot(hb, w1_ref[:, C_GR:C_GR + GLA_WIDTH])
    sg_ref[0] = zgr * _sigmoid(zgr)

    for br in range(N_BRANCH):
        lo = C_GATE + br * D_MODEL
        zg = _dot(hb, w1_ref[:, lo:lo + D_MODEL])
        gate_ref[0, :, br * D_MODEL:(br + 1) * D_MODEL] = _sigmoid(zg).astype(BF16)


def _inproj_call(x, mod, lw, tabs, *, rope, emit_ctx, tm):
    nb, seq, _ = x.shape
    nt = seq // tm
    mod_idx = (lambda b, t: (b, 0, 0)) if mod.shape[0] == nb else (lambda b, t: (0, 0, 0))
    tok = lambda w: pl.BlockSpec((1, tm, w), lambda b, t: (b, t, 0))

    ins = [x, mod, lw["g1"], lw["w1"], lw["dft64"], lw["gq"], lw["wq"]]
    specs = [tok(D_MODEL), pl.BlockSpec((1, N_MOD, D_MODEL), mod_idx),
             _const_spec((1, D_MODEL)), _const_spec((D_MODEL, W1_COLS)),
             _const_spec((F_WIDTH, 2 * F_WIDTH)), _const_spec((1, Q_LORA)),
             _const_spec((Q_LORA, MLA_HEADS * HEAD_PAD))]
    if rope:
        ins.append(lw["wqr"])
        specs.append(_const_spec((Q_LORA, MLA_HEADS * HEAD_PAD)))
    ins += [lw["gkv"], lw["wkt"], lw["wv"], lw["wkrt"], lw["wa"], lw["ba"]]
    specs += [_const_spec((1, KV_LORA)), _const_spec((MLA_HEADS * QK_NOPE, KV_LORA)),
              _const_spec((KV_LORA, MLA_WIDTH)), _const_spec((2 * QK_ROPE, D_MODEL)),
              _const_spec((LANES, 2 * GLA_QK)), _const_spec((1, 2 * GLA_QK))]
    if rope:
        ins += [tabs["cos_q"], tabs["sin_q"], tabs["cos_t"], tabs["sin_t"]]
        specs += [pl.BlockSpec((tm, HEAD_PAD), lambda b, t: (t, 0)),
                  pl.BlockSpec((tm, HEAD_PAD), lambda b, t: (t, 0)),
                  pl.BlockSpec((QK_ROPE, tm), lambda b, t: (0, t)),
                  pl.BlockSpec((QK_ROPE, tm), lambda b, t: (0, t))]

    def sds(w, dt):
        return jax.ShapeDtypeStruct((nb, seq, w), dt)

    out_shape = [sds(2 * F_WIDTH, BF16), sds(MLA_HEADS * HEAD_PAD, BF16),
                 jax.ShapeDtypeStruct((nb, MLA_HEADS, HEAD_PAD, seq), BF16),
                 sds(MLA_WIDTH, BF16), sds(2 * GLA_QK, F32), sds(GLA_WIDTH, F32),
                 sds(2 * GLA_QK, F32), sds(GLA_WIDTH, F32), sds(N_BRANCH * D_MODEL, BF16)]
    out_specs = [tok(2 * F_WIDTH), tok(MLA_HEADS * HEAD_PAD),
                 pl.BlockSpec((1, MLA_HEADS, HEAD_PAD, tm), lambda b, t: (b, 0, 0, t)),
                 tok(MLA_WIDTH), tok(2 * GLA_QK), tok(GLA_WIDTH), tok(2 * GLA_QK), tok(GLA_WIDTH),
                 tok(N_BRANCH * D_MODEL)]
    if emit_ctx:
        out_shape += [sds(KV_LORA, F32), sds(QK_ROPE, F32)]
        out_specs += [tok(KV_LORA), tok(QK_ROPE)]

    return pl.pallas_call(
        functools.partial(_inproj_kernel, rope=rope, emit_ctx=emit_ctx),
        grid=(nb, nt), in_specs=specs, out_specs=out_specs, out_shape=out_shape,
        compiler_params=_params(("parallel", "parallel")),
        name="in_proj_lat" if rope else "in_proj_ctx",
    )(*ins)


def _kvcache_kernel(ckv_ref, krt_ref, wkt_ref, wv_ref, kt_ref, v_ref):
    ckvb = ckv_ref[0].astype(BF16)
    knt = _dot_nt(wkt_ref[...], ckvb)
    v_ref[0] = _dot(ckvb, wv_ref[...]).astype(BF16)
    krt = krt_ref[0].astype(BF16)
    zero_pad = jnp.zeros((HEAD_PAD - QK_NOPE - QK_ROPE, krt.shape[1]), BF16)
    for hd in range(MLA_HEADS):
        kt_ref[0, hd, 0:QK_NOPE, :] = knt[hd * QK_NOPE:(hd + 1) * QK_NOPE].astype(BF16)
        kt_ref[0, hd, QK_NOPE:QK_NOPE + QK_ROPE, :] = krt
        kt_ref[0, hd, QK_NOPE + QK_ROPE:HEAD_PAD, :] = zero_pad


def _kvcache_call(ckv, krt, lw):
    nb, past, _ = ckv.shape
    return pl.pallas_call(
        _kvcache_kernel,
        grid=(nb,),
        in_specs=[pl.BlockSpec((1, past, KV_LORA), lambda b: (b, 0, 0)),
                  pl.BlockSpec((1, QK_ROPE, past), lambda b: (b, 0, 0)),
                  _const_spec((MLA_HEADS * QK_NOPE, KV_LORA)), _const_spec((KV_LORA, MLA_WIDTH))],
        out_specs=[pl.BlockSpec((1, MLA_HEADS, HEAD_PAD, past), lambda b: (b, 0, 0, 0)),
                   pl.BlockSpec((1, past, MLA_WIDTH), lambda b: (b, 0, 0))],
        out_shape=[jax.ShapeDtypeStruct((nb, MLA_HEADS, HEAD_PAD, past), BF16),
                   jax.ShapeDtypeStruct((nb, past, MLA_WIDTH), BF16)],
        compiler_params=_params(("parallel",)),
        name="kv_cache",
    )(ckv, krt, lw["wkt"], lw["wv"])


def _fourier_kernel(c_ref, ns_ref, fcs_ref, o_ref, *, norm):
    fc = fcs_ref[0, :, 0:F_WIDTH]
    fs = fcs_ref[0, :, F_WIDTH:2 * F_WIDTH]
    o_ref[0] = ((_dot(c_ref[...], fc) + _dot(ns_ref[...], fs)) * norm).astype(BF16)


def _fourier_call(fcs, dft_c, dft_ns, *, tr):
    nb, seq, _ = fcs.shape
    norm = 1.0 / math.sqrt(seq * F_GROUP_DIM)
    return pl.pallas_call(
        functools.partial(_fourier_kernel, norm=norm),
        grid=(seq // tr, nb),
        in_specs=[pl.BlockSpec((tr, seq), lambda i, b: (i, 0)),
                  pl.BlockSpec((tr, seq), lambda i, b: (i, 0)),
                  pl.BlockSpec((1, seq, 2 * F_WIDTH), lambda i, b: (b, 0, 0))],
        out_specs=pl.BlockSpec((1, tr, F_WIDTH), lambda i, b: (b, i, 0)),
        out_shape=jax.ShapeDtypeStruct((nb, seq, F_WIDTH), BF16),
        compiler_params=_params(("parallel", "parallel")),
        name="fourier",
    )(dft_c, dft_ns, fcs)


def _attn_kernel(*refs, has_cache, ck):
    if has_cache:
        q_ref, kt_ref, v_ref, ktc_ref, vc_ref, o_ref, s_scr = refs
    else:
        q_ref, kt_ref, v_ref, o_ref, s_scr = refs
        ktc_ref = vc_ref = None

    chunks = []
    col = 0
    if has_cache:
        past = ktc_ref.shape[3]
        for off in range(0, past, ck):
            w = min(ck, past - off)
            chunks.append((ktc_ref, vc_ref, off, col, w))
            col += w
    seq = kt_ref.shape[3]
    for off in range(0, seq, ck):
        w = min(ck, seq - off)
        chunks.append((kt_ref, v_ref, off, col, w))
        col += w

    outs = []
    for j in range(HEADS_PER_STEP):
        qh = q_ref[0, :, j * HEAD_PAD:(j + 1) * HEAD_PAD]
        m_t = None
        for (k_r, _, off, c0, w) in chunks:
            s = _dot(qh, k_r[0, j, :, off:off + w])
            s_scr[:, c0:c0 + w] = s
            for t in range(0, w, LANES):
                piece = s[:, t:t + LANES]
                m_t = piece if m_t is None else jnp.maximum(m_t, piece)
        m = jnp.max(m_t, axis=-1, keepdims=True)
        l_t = None
        acc = None
        for (_, v_r, off, c0, w) in chunks:
            p = jnp.exp(s_scr[:, c0:c0 + w] - m)
            for t in range(0, w, LANES):
                piece = p[:, t:t + LANES]
                l_t = piece if l_t is None else l_t + piece
            pv = _dot(p.astype(BF16), v_r[0, off:off + w, :])
            acc = pv if acc is None else acc + pv
        l = jnp.sum(l_t, axis=-1, keepdims=True)
        outs.append(acc / l)
    lane = lax.broadcasted_iota(jnp.int32, outs[0].shape, 1)
    o_ref[0] = jnp.where(lane < V_DIM, outs[0], outs[1]).astype(BF16)


def _attn_call(q, kt, v, cache, *, tq, ck):
    nb, seq, _ = q.shape
    has_cache = cache is not None
    n_keys = seq + (cache[0].shape[3] if has_cache else 0)
    pair_w = HEADS_PER_STEP * V_DIM
    ins = [q, kt, v]
    specs = [pl.BlockSpec((1, tq, HEADS_PER_STEP * HEAD_PAD), lambda b, p, i: (b, i, p)),
             pl.BlockSpec((1, HEADS_PER_STEP, HEAD_PAD, seq), lambda b, p, i: (b, p, 0, 0)),
             pl.BlockSpec((1, seq, pair_w), lambda b, p, i: (b, 0, p))]
    if has_cache:
        past = cache[0].shape[3]
        ins += list(cache)
        specs += [pl.BlockSpec((1, HEADS_PER_STEP, HEAD_PAD, past), lambda b, p, i: (b, p, 0, 0)),
                  pl.BlockSpec((1, past, pair_w), lambda b, p, i: (b, 0, p))]
    return pl.pallas_call(
        functools.partial(_attn_kernel, has_cache=has_cache, ck=ck),
        grid=(nb, MLA_HEADS // HEADS_PER_STEP, seq // tq),
        in_specs=specs,
        out_specs=pl.BlockSpec((1, tq, pair_w), lambda b, p, i: (b, i, p)),
        out_shape=jax.ShapeDtypeStruct((nb, seq, MLA_WIDTH), BF16),
        scratch_shapes=[pltpu.VMEM((tq, n_keys), F32)],
        compiler_params=_params(("parallel", "parallel", "parallel")),
        name="attention_lat" if has_cache else "attention_ctx",
    )(*ins)


def _gla_tile(la, q, k, v, st_ref, cum_ref, tot_ref, e2_ref, mask_ref, *, reverse):
    cum = _dot_exact_lhs(cum_ref[...], la)
    tot = _dot_exact_lhs(tot_ref[...], la)
    qe = (q * jnp.exp(cum)).astype(BF16)
    ke = (k * jnp.exp(tot - cum)).astype(BF16)
    dec = jnp.exp(tot)
    jrow = lax.broadcasted_iota(jnp.int32, (GLA_SUB, GLA_QK), 0)
    n_sub = GLA_TILE // GLA_SUB
    order = range(n_sub - 1, -1, -1) if reverse else range(n_sub)
    outs = [None] * n_sub
    for n in order:
        r = slice(n * GLA_SUB, (n + 1) * GLA_SUB)
        bq, qq, kk, vv = cum[r], q[r], k[r], v[r]
        rows = []
        for i in range(GLA_SUB):
            d = jnp.minimum(bq[i:i + 1] - bq, 0.0)
            w = qq[i:i + 1] * kk * jnp.exp(d)
            keep = (jrow >= i) if reverse else (jrow <= i)
            rows.append(jnp.where(keep, w, 0.0))
        wmat = jnp.concatenate(rows, axis=0).astype(BF16)
        a = _dot(wmat, e2_ref[...])
        intra = jnp.sum(a.reshape(GLA_SUB, GLA_SUB, GLA_WIDTH) * vv[None], axis=1)
        st = st_ref[...]
        inter = _dot_nt(qe[r], st.astype(BF16))
        outs[n] = inter + intra
        upd = _dot_tn(vv.astype(BF16), ke[r])
        st_ref[...] = st * dec[n * GLA_SUB:n * GLA_SUB + 1] + upd * mask_ref[...]
    return jnp.concatenate(outs, axis=0)


def _gla_kernel(qk_ref, gv_ref, la_ref, sg_ref, gn_ref, s0_ref, tril_ref, triu_ref, tot_ref, e2_ref,
                e3_ref, mask_ref, og_ref, sout_ref, of_scr, st_scr):
    seq = qk_ref.shape[1]
    n_tiles = seq // GLA_TILE

    def load(t, lo):
        rows = pl.ds(pl.multiple_of(t * GLA_TILE, GLA_TILE), GLA_TILE)
        return (la_ref[0, rows, lo:lo + GLA_QK], qk_ref[0, rows, 0:GLA_QK],
                qk_ref[0, rows, GLA_QK:2 * GLA_QK], gv_ref[0, rows, :], rows)

    st_scr[...] = s0_ref[0, 0]

    def fwd(t, carry):
        la, q, k, v, rows = load(t, 0)
        of_scr[rows, :] = _gla_tile(la, q, k, v, st_scr, tril_ref, tot_ref, e2_ref, mask_ref, reverse=False)
        return carry

    lax.fori_loop(0, n_tiles, fwd, 0)
    sout_ref[0, 0] = st_scr[...]
    st_scr[...] = s0_ref[0, 1]

    def bwd(i, carry):
        t = n_tiles - 1 - i
        la, q, k, v, rows = load(t, GLA_QK)
        o = of_scr[rows, :] + _gla_tile(la, q, k, v, st_scr, triu_ref, tot_ref, e2_ref, mask_ref, reverse=True)
        o2 = o * o
        hi = o2.astype(BF16)
        lo = (o2 - hi.astype(F32)).astype(BF16)
        ms = _dot(hi, e3_ref[...]) + _dot(lo, e3_ref[...])
        og = o * lax.rsqrt(ms + NORM_EPS) * gn_ref[...] * sg_ref[0, rows, :]
        og_ref[0, rows, :] = og.astype(BF16)
        return carry

    lax.fori_loop(0, n_tiles, bwd, 0)
    sout_ref[0, 1] = st_scr[...]


def _gla_call(qk, gv, la, sg, gnorm, s0t, consts):
    nb, seq, _ = qk.shape
    tok = lambda w: pl.BlockSpec((1, seq, w), lambda b: (b, 0, 0))
    st_spec = pl.BlockSpec((1, 2, GLA_WIDTH, GLA_QK), lambda b: (b, 0, 0, 0))
    return pl.pallas_call(
        _gla_kernel,
        grid=(nb,),
        in_specs=[tok(2 * GLA_QK), tok(GLA_WIDTH), tok(2 * GLA_QK), tok(GLA_WIDTH),
                  _const_spec((1, GLA_WIDTH)), st_spec,
                  _const_spec((GLA_TILE, GLA_TILE)), _const_spec((GLA_TILE, GLA_TILE)),
                  _const_spec((GLA_TILE, GLA_TILE)), _const_spec((GLA_QK, GLA_WIDTH)),
                  _const_spec((GLA_WIDTH, GLA_WIDTH)), _const_spec((GLA_WIDTH, GLA_QK))],
        out_specs=[tok(GLA_WIDTH), st_spec],
        out_shape=[jax.ShapeDtypeStruct((nb, seq, GLA_WIDTH), BF16),
                   jax.ShapeDtypeStruct((nb, 2, GLA_WIDTH, GLA_QK), F32)],
        scratch_shapes=[pltpu.VMEM((seq, GLA_WIDTH), F32), pltpu.VMEM((GLA_WIDTH, GLA_QK), F32)],
        compiler_params=_params(("parallel",)),
        name="gla",
    )(qk, gv, la, sg, gnorm, s0t, consts["tril"], consts["triu"], consts["tot"], consts["e2"],
      consts["e3"], consts["mask"])


def _tail_kernel(x_ref, f_ref, at_ref, og_ref, gate_ref, mod_ref, wof_ref, wom_ref, wog_ref, wout_ref,
                 g2_ref, w1_ref, w2_ref, gf_ref, o_ref, *, final, ff_chunk):
    mod = mod_ref[0]
    gate1, shift2, scale2, gate2 = mod[2:3], mod[3:4], mod[4:5], mod[5:6]
    ya = _dot(f_ref[0], wof_ref[...])
    yb = _dot(at_ref[0], wom_ref[...])
    yc = _dot(og_ref[0], wog_ref[...])
    merged = (gate_ref[0, :, 0:D_MODEL].astype(F32) * ya
              + gate_ref[0, :, D_MODEL:2 * D_MODEL].astype(F32) * yb
              + gate_ref[0, :, 2 * D_MODEL:3 * D_MODEL].astype(F32) * yc)
    x1 = x_ref[0] + gate1 * _dot(merged.astype(BF16), wout_ref[...])
    h2 = (_rms(x1) * g2_ref[...] * (1.0 + scale2) + shift2).astype(BF16)
    acc = None
    for c0 in range(0, D_FF, ff_chunk):
        u = jnp.maximum(_dot(h2, w1_ref[:, c0:c0 + ff_chunk]), 0.0)
        part = _dot((u * u).astype(BF16), w2_ref[c0:c0 + ff_chunk, :])
        acc = part if acc is None else acc + part
    x2 = x1 + gate2 * acc
    if final:
        x2 = _rms(x2) * gf_ref[...]
    o_ref[0] = x2


def _tail_call(x, f, attn, og, gates, mod, lw, final_g, *, final, tm):
    nb, seq, _ = x.shape
    mod_idx = (lambda b, t: (b, 0, 0)) if mod.shape[0] == nb else (lambda b, t: (0, 0, 0))
    tok = lambda w: pl.BlockSpec((1, tm, w), lambda b, t: (b, t, 0))
    return pl.pallas_call(
        functools.partial(_tail_kernel, final=final, ff_chunk=1024),
        grid=(nb, seq // tm),
        in_specs=[tok(D_MODEL), tok(F_WIDTH), tok(MLA_WIDTH), tok(GLA_WIDTH), tok(N_BRANCH * D_MODEL),
                  pl.BlockSpec((1, N_MOD, D_MODEL), mod_idx),
                  _const_spec((F_WIDTH, D_MODEL)), _const_spec((MLA_WIDTH, D_MODEL)),
                  _const_spec((GLA_WIDTH, D_MODEL)), _const_spec((D_MODEL, D_MODEL)),
                  _const_spec((1, D_MODEL)), _const_spec((D_MODEL, D_FF)), _const_spec((D_FF, D_MODEL)),
                  _const_spec((1, D_MODEL))],
        out_specs=tok(D_MODEL),
        out_shape=jax.ShapeDtypeStruct((nb, seq, D_MODEL), F32),
        compiler_params=_params(("parallel", "parallel")),
        name="tail",
    )(x, f, attn, og, gates, mod, lw["wof"], lw["wom"], lw["wog"], lw["wout"], lw["g2"], lw["wff1"],
      lw["wff2"], final_g)


def _rot_partner(w):
    q = QK_ROPE // 4
    return jnp.concatenate([-w[..., q:2 * q], w[..., 0:q], -w[..., 3 * q:4 * q], w[..., 2 * q:3 * q]], axis=-1)


def _rope_tables(seq):
    half = QK_ROPE // 2
    pos = jnp.arange(seq)
    row = (pos // GRID_W).astype(F32)
    col = (pos % GRID_W).astype(F32)
    inv = ROPE_BASE ** (-jnp.arange(0, half, 2, dtype=F32) / half)
    ang = jnp.concatenate([row[:, None] * inv, row[:, None] * inv, col[:, None] * inv, col[:, None] * inv], axis=1)
    cos, sin = jnp.cos(ang), jnp.sin(ang)
    pad = HEAD_PAD - QK_NOPE - QK_ROPE
    cos_q = jnp.concatenate([jnp.ones((seq, QK_NOPE), F32), cos, jnp.zeros((seq, pad), F32)], axis=1)
    sin_q = jnp.concatenate([jnp.zeros((seq, QK_NOPE), F32), sin, jnp.zeros((seq, pad), F32)], axis=1)
    return dict(cos_q=cos_q, sin_q=sin_q, cos_t=cos.T, sin_t=sin.T)


def _dft_tables(seq):
    k = jnp.arange(seq, dtype=jnp.int32)
    ph = (k[:, None] * k[None, :]) % seq
    ang = ph.astype(F32) * (2.0 * math.pi / seq)
    return jnp.cos(ang).astype(BF16), (-jnp.sin(ang)).astype(BF16)


def _dft64():
    k = jnp.arange(F_GROUP_DIM, dtype=jnp.int32)
    ang = ((k[:, None] * k[None, :]) % F_GROUP_DIM).astype(F32) * (2.0 * math.pi / F_GROUP_DIM)
    eye = jnp.eye(F_GROUPS, dtype=F32)
    return jnp.concatenate([jnp.kron(eye, jnp.cos(ang)), jnp.kron(eye, jnp.sin(ang))], axis=1).astype(BF16)


def _gla_consts():
    idx = np.arange(GLA_TILE)
    same = (idx[:, None] // GLA_SUB) == (idx[None, :] // GLA_SUB)
    tril = same & (idx[None, :] <= idx[:, None])
    triu = same & (idx[None, :] >= idx[:, None])
    hq = np.arange(GLA_QK) // GLA_DK
    hv = np.arange(GLA_WIDTH) // GLA_DV
    e2 = hq[:, None] == hv[None, :]
    e3 = (hv[:, None] == hv[None, :]) / GLA_DV
    mask = hv[:, None] == hq[None, :]
    return dict(tril=jnp.asarray(tril, BF16), triu=jnp.asarray(triu, BF16), tot=jnp.asarray(same, BF16),
                e2=jnp.asarray(e2, BF16), e3=jnp.asarray(e3, BF16), mask=jnp.asarray(mask, F32))


def _layer_weights(l, w_in, norm1_g, mla_q_norm_g, mla_kv_norm_g, w_q_up, w_kv_up, gla_wa2_f, gla_ba_f,
                   gla_wa2_b, gla_ba_b, gla_norm_g, w_o_fourier, w_o_mla, w_o_gla, w_out, norm2_g, w_ff1,
                   w_ff2, dft64):
    wi = w_in[l]
    o = np.cumsum([0, F_WIDTH, Q_LORA, KV_LORA, QK_ROPE, GLA_QK, GLA_QK, GLA_WIDTH, GLA_WIDTH,
                   GLA_GATE_RANK, GLA_GATE_RANK, N_BRANCH * D_MODEL])
    zf, zqd, zkvd, zkr, zgq, zgk, zgv, zgr, zaf, zab, zgate = (wi[:, int(o[i]):int(o[i + 1])] for i in range(11))
    small = jnp.concatenate([zkr, jnp.zeros((D_MODEL, S_AF - QK_ROPE), F32), zaf, zab,
                             jnp.zeros((D_MODEL, LANES - S_AB - GLA_GATE_RANK), F32)], axis=1)
    w1 = jnp.concatenate([zf, zqd, zkvd, zgq, zgk, zgv, zgr, zgate, small], axis=1).astype(BF16)
    wkrt = jnp.concatenate([zkr, _rot_partner(zkr)], axis=1).T.astype(BF16)

    wq3 = w_q_up[l].reshape(Q_LORA, MLA_HEADS, QK_NOPE + QK_ROPE)
    pad = jnp.zeros((Q_LORA, MLA_HEADS, HEAD_PAD - QK_NOPE - QK_ROPE), F32)
    wq = jnp.concatenate([wq3, pad], axis=2).reshape(Q_LORA, -1).astype(BF16)
    wqr = jnp.concatenate([jnp.zeros((Q_LORA, MLA_HEADS, QK_NOPE), F32), _rot_partner(wq3[..., QK_NOPE:]), pad],
                          axis=2).reshape(Q_LORA, -1).astype(BF16)
    wkv3 = w_kv_up[l].reshape(KV_LORA, MLA_HEADS, QK_NOPE + V_DIM)
    wkt = wkv3[..., :QK_NOPE].reshape(KV_LORA, -1).T.astype(BF16)
    wv = wkv3[..., QK_NOPE:].reshape(KV_LORA, -1).astype(BF16)

    wa = jnp.zeros((LANES, 2 * GLA_QK), F32)
    wa = wa.at[S_AF:S_AF + GLA_GATE_RANK, 0:GLA_QK].set(gla_wa2_f[l])
    wa = wa.at[S_AB:S_AB + GLA_GATE_RANK, GLA_QK:].set(gla_wa2_b[l]).astype(BF16)
    ba = jnp.concatenate([gla_ba_f[l], gla_ba_b[l]])[None]
    return dict(
        g1=norm1_g[l][None], w1=w1, dft64=dft64, gq=mla_q_norm_g[l][None], wq=wq, wqr=wqr,
        gkv=mla_kv_norm_g[l][None], wkt=wkt, wv=wv, wkrt=wkrt, wa=wa, ba=ba, gn=gla_norm_g[l][None],
        wof=w_o_fourier[l].astype(BF16), wom=w_o_mla[l].astype(BF16), wog=w_o_gla[l].astype(BF16),
        wout=w_out[l].astype(BF16), g2=norm2_g[l][None], wff1=w_ff1[l].astype(BF16),
        wff2=w_ff2[l].astype(BF16))


def _state_to_blockdiag_t(s):
    nb = s.shape[0]
    eye = jnp.eye(GLA_HEADS, dtype=s.dtype)
    st = jnp.einsum("bxhde,hg->bxhegd", s, eye)
    return st.reshape(nb, 2, GLA_WIDTH, GLA_QK)


def _blockdiag_t_to_state(st):
    nb = st.shape[0]
    s6 = st.reshape(nb, 2, GLA_HEADS, GLA_DV, GLA_HEADS, GLA_DK)
    diag = jnp.stack([s6[:, :, h, :, h, :] for h in range(GLA_HEADS)], axis=2)
    return jnp.swapaxes(diag, -1, -2)


def _pick_tile(seq, target):
    t = min(seq, target)
    assert seq % t == 0 and t % GLA_SUB == 0, (seq, t)
    return t


def _mixer_and_tail(x, mod, lw, tabs, dft, gla_c, cache, s0t, final_g, *, rope, final):
    nb, seq, _ = x.shape
    tm = _pick_tile(seq, 256)
    outs = _inproj_call(x, mod, lw, tabs, rope=rope, emit_ctx=not rope, tm=tm)
    fcs, q, kt, v, qk, gv, la, sg, gates = outs[:9]
    f = _fourier_call(fcs, dft[0], dft[1], tr=_pick_tile(seq, 512))
    attn = _attn_call(q, kt, v, cache, tq=_pick_tile(seq, 256), ck=512)
    og, st_out = _gla_call(qk, gv, la, sg, lw["gn"], s0t, gla_c)
    y = _tail_call(x, f, attn, og, gates, mod, lw, final_g, final=final, tm=tm)
    return y, outs[9:], st_out


def kernel(x_prompt, x_sample, c, cache_mla_ckv, cache_mla_krope, state_gla, c_ctx, w_mod, b_mod, norm1_g, w_in, mla_q_norm_g, mla_kv_norm_g, w_q_up, w_kv_up, gla_wa2_f, gla_ba_f, gla_wa2_b, gla_ba_b, gla_norm_g, w_o_fourier, w_o_mla, w_o_gla, w_out, norm2_g, w_ff1, w_ff2, final_norm_g):
    depth = w_mod.shape[0]
    n_ctx, seq_ctx, _ = x_prompt.shape
    n_lat, seq_lat, _ = x_sample.shape
    assert seq_ctx % GLA_TILE == 0 and seq_lat % GLA_TILE == 0 and seq_lat % GRID_W == 0
    assert 1 + n_lat <= MOD_ROWS

    cond = jnp.concatenate([c_ctx[None], c, jnp.zeros((MOD_ROWS - 1 - n_lat, D_MODEL), F32)], axis=0)
    mod_all = _mod_call(cond, w_mod, b_mod).reshape(depth, MOD_ROWS, N_MOD, D_MODEL)

    dft64 = _dft64()
    gla_c = _gla_consts()
    tabs = _rope_tables(seq_lat)
    dft_ctx = _dft_tables(seq_ctx)
    dft_lat = _dft_tables(seq_lat)
    final_g = final_norm_g[None]
    zero_state = jnp.zeros((n_ctx, 2, GLA_WIDTH, GLA_QK), F32)

    xp, xs = x_prompt, x_sample
    ckv_list, kr_list, gla_list = [], [], []
    for l in range(depth):
        lw = _layer_weights(l, w_in, norm1_g, mla_q_norm_g, mla_kv_norm_g, w_q_up, w_kv_up, gla_wa2_f,
                            gla_ba_f, gla_wa2_b, gla_ba_b, gla_norm_g, w_o_fourier, w_o_mla, w_o_gla,
                            w_out, norm2_g, w_ff1, w_ff2, dft64)
        final = l == depth - 1
        mod_ctx = mod_all[l, 0:1]
        mod_lat = mod_all[l, 1:1 + n_lat]

        xp, (ckv_c, kr_c), st_c = _mixer_and_tail(xp, mod_ctx, lw, None, dft_ctx, gla_c, None, zero_state,
                                                  final_g, rope=False, final=final)
        ckv_list.append(ckv_c)
        kr_list.append(kr_c)
        gla_list.append(_blockdiag_t_to_state(st_c))

        cache = _kvcache_call(cache_mla_ckv[:, l], jnp.swapaxes(cache_mla_krope[:, l], 1, 2), lw)
        s0t = _state_to_blockdiag_t(state_gla[:, l].astype(F32))
        xs, _, _ = _mixer_and_tail(xs, mod_lat, lw, tabs, dft_lat, gla_c, cache, s0t, final_g,
                                   rope=True, final=final)

    return (xp, xs, jnp.stack(ckv_list, axis=1), jnp.stack(kr_list, axis=1), jnp.stack(gla_list, axis=1))
```

```python
import functools
import math

import numpy as np
import jax
import jax.numpy as jnp
from jax import lax
from jax.experimental import pallas as pl
from jax.experimental.pallas import tpu as pltpu

F32 = jnp.float32
BF16 = jnp.bfloat16

D_MODEL = 1024
GRID_W = 64
F_GROUPS = 4
F_GROUP_DIM = 64
F_WIDTH = F_GROUPS * F_GROUP_DIM
MLA_HEADS = 8
QK_NOPE = 64
QK_ROPE = 32
V_DIM = 64
Q_LORA = 256
KV_LORA = 128
MLA_WIDTH = MLA_HEADS * V_DIM
GLA_HEADS = 4
GLA_DK = 32
GLA_DV = 64
GLA_WIDTH = GLA_HEADS * GLA_DV
GLA_QK = GLA_HEADS * GLA_DK
GLA_GATE_RANK = 16
GLA_TAU = 16.0
D_FF = 4 * D_MODEL
N_BRANCH = 3
ROPE_BASE = 10000.0
NORM_EPS = 1e-6
N_MOD = 6
LOG2_E = math.log2(math.e)

LANES = 128
VMEM_LIMIT_BYTES = 56 * 1024 * 1024

HEAD_PAD = LANES
HEADS_PER_STEP = 2
ATTN_SLOTS = 3
GLA_SUB = 16
GLA_TILE = 128
MOD_ROWS = 16
GLA_MASKED_LOG2 = -1e30

C_F = 0
C_QD = C_F + F_WIDTH
C_KVD = C_QD + Q_LORA
C_GQ = C_KVD + KV_LORA
C_GK = C_GQ + GLA_QK
C_SMALL = C_GK + GLA_QK
C_GV = C_SMALL + LANES
C_GR = C_GV + GLA_WIDTH
C_GATE = C_GR + GLA_WIDTH
W1_COLS = C_GATE + N_BRANCH * D_MODEL
S_KR = 0
S_AF = 64
S_AB = 80


def _dot(a, b):
    return jnp.dot(a, b, preferred_element_type=F32)


def _dot_nt(a, b):
    return lax.dot_general(a, b, (((1,), (1,)), ((), ())), preferred_element_type=F32)


def _dot_tn(a, b):
    return lax.dot_general(a, b, (((0,), (0,)), ((), ())), preferred_element_type=F32)


def _split3(x):
    hi = x.astype(BF16)
    r1 = x - hi.astype(F32)
    mid = r1.astype(BF16)
    lo = (r1 - mid.astype(F32)).astype(BF16)
    return hi, mid, lo


def _dot_exact_lhs(m, x):
    hi, mid, lo = _split3(x)
    return _dot(m, hi) + _dot(m, mid) + _dot(m, lo)


def _rms(x):
    return x * lax.rsqrt(jnp.mean(x * x, axis=-1, keepdims=True) + NORM_EPS)


def _sigmoid(x):
    return 1.0 / (1.0 + jnp.exp(-x))


def _const_spec(shape):
    return pl.BlockSpec(shape, lambda *_: (0,) * len(shape))


def _params(sem):
    return pltpu.CompilerParams(dimension_semantics=sem, vmem_limit_bytes=VMEM_LIMIT_BYTES)


def _mod_kernel(c_ref, w_ref, b_ref, o_ref):
    c = c_ref[...]
    s = (c * _sigmoid(c)).astype(BF16)
    o_ref[0] = _dot(s, w_ref[0].astype(BF16)) + b_ref[0]


def _mod_call(cond, w_mod, b_mod):
    depth = w_mod.shape[0]
    n_col = w_mod.shape[2] // D_MODEL
    return pl.pallas_call(
        _mod_kernel,
        grid=(depth, n_col),
        in_specs=[
            pl.BlockSpec((MOD_ROWS, D_MODEL), lambda l, j: (0, 0)),
            pl.BlockSpec((1, D_MODEL, D_MODEL), lambda l, j: (l, 0, j)),
            pl.BlockSpec((1, 1, D_MODEL), lambda l, j: (l, 0, j)),
        ],
        out_specs=pl.BlockSpec((1, MOD_ROWS, D_MODEL), lambda l, j: (l, 0, j)),
        out_shape=jax.ShapeDtypeStruct((depth, MOD_ROWS, w_mod.shape[2]), F32),
        compiler_params=_params(("arbitrary", "arbitrary")),
        name="mod_proj",
    )(cond, w_mod, b_mod.reshape(depth, 1, -1))


def _inproj_kernel(*refs, rope, emit_ctx):
    it = iter(refs)
    x_ref, mod_ref, g1_ref, w1_ref, dft_ref, gq_ref, wq_ref = (next(it) for _ in range(7))
    wqr_ref = next(it) if rope else None
    gkv_ref, wkt_ref, wv_ref, wkrt_ref, wa_ref, ba_ref = (next(it) for _ in range(6))
    if rope:
        cq_ref, sq_ref, ct_ref, st_ref = (next(it) for _ in range(4))
    fcs_ref, q_ref, kt_ref, v_ref, qk_ref, gv_ref, la_ref, sg_ref, gate_ref = (next(it) for _ in range(9))
    if emit_ctx:
        ckv_ref, kr_ref = next(it), next(it)

    x = x_ref[0]
    mod = mod_ref[0]
    shift1, scale1 = mod[0:1], mod[1:2]
    h = _rms(x) * g1_ref[...] * (1.0 + scale1) + shift1
    hb = h.astype(BF16)

    def branch_gate(br):
        lo = C_GATE + br * D_MODEL
        zg = _dot(hb, w1_ref[:, lo:lo + D_MODEL])
        gate_ref[0, :, br * D_MODEL:(br + 1) * D_MODEL] = _sigmoid(zg).astype(BF16)

    z_qd = _dot(hb, w1_ref[:, C_QD:C_QD + Q_LORA])
    z_kg = _dot(hb, w1_ref[:, C_KVD:C_KVD + KV_LORA + GLA_QK])
    z_ks = _dot(hb, w1_ref[:, C_GK:C_GK + GLA_QK + LANES])
    krt2 = _dot_nt(wkrt_ref[...], hb)
    branch_gate(0)

    cq = (_rms(z_qd) * gq_ref[...]).astype(BF16)
    qa = _dot(cq, wq_ref[...])
    scale = (QK_NOPE + QK_ROPE) ** -0.5 * LOG2_E
    if rope:
        qb = _dot(cq, wqr_ref[...])
        cos_q, sin_q = cq_ref[...], sq_ref[...]
        for hd in range(MLA_HEADS):
            sl = slice(hd * HEAD_PAD, (hd + 1) * HEAD_PAD)
            q_ref[0, :, sl] = ((qa[:, sl] * cos_q + qb[:, sl] * sin_q) * scale).astype(BF16)
    else:
        q_ref[0] = (qa * scale).astype(BF16)

    ckv = _rms(z_kg[:, 0:KV_LORA]) * gkv_ref[...]
    ckvb = ckv.astype(BF16)
    knt = _dot_nt(wkt_ref[...], ckvb)
    v_ref[0] = _dot(ckvb, wv_ref[...]).astype(BF16)
    krt = krt2[0:QK_ROPE]
    if rope:
        krt = krt * ct_ref[...] + krt2[QK_ROPE:2 * QK_ROPE] * st_ref[...]
    krt = krt.astype(BF16)
    zero_pad = jnp.zeros((HEAD_PAD - QK_NOPE - QK_ROPE, krt.shape[1]), BF16)
    for hd in range(MLA_HEADS):
        kt_ref[0, hd, 0:QK_NOPE, :] = knt[hd * QK_NOPE:(hd + 1) * QK_NOPE].astype(BF16)
        kt_ref[0, hd, QK_NOPE:QK_NOPE + QK_ROPE, :] = krt
        kt_ref[0, hd, QK_NOPE + QK_ROPE:HEAD_PAD, :] = zero_pad
    small = z_ks[:, GLA_QK:GLA_QK + LANES]
    if emit_ctx:
        ckv_ref[0] = ckv
        kr_ref[0] = small[:, S_KR:S_KR + QK_ROPE]
    branch_gate(1)

    zf = _dot(hb, w1_ref[:, C_F:C_F + F_WIDTH])

    qk_ref[0, :, 0:GLA_QK] = z_kg[:, KV_LORA:KV_LORA + GLA_QK] * (GLA_DK ** -0.5)
    qk_ref[0, :, GLA_QK:2 * GLA_QK] = z_ks[:, 0:GLA_QK]
    gv_ref[0] = _dot(hb, w1_ref[:, C_GV:C_GV + GLA_WIDTH])
    a_pre = _dot(small.astype(BF16), wa_ref[...]) + ba_ref[...]
    log_sig = jnp.minimum(a_pre, 0.0) - jnp.log(1.0 + jnp.exp(-jnp.abs(a_pre)))
    la_ref[0] = log_sig * (1.0 / GLA_TAU)
    zgr = _dot(hb, w1_ref[:, C_GR:C_GR + GLA_WIDTH])
    sg_ref[0] = zgr * _sigmoid(zgr)
    fcs_ref[0] = _dot(zf.astype(BF16), dft_ref[...]).astype(BF16)
    branch_gate(2)


def _inproj_call(x, mod, lw, tabs, *, rope, emit_ctx, tm):
    nb, seq, _ = x.shape
    nt = seq // tm
    mod_idx = (lambda b, t: (b, 0, 0)) if mod.shape[0] == nb else (lambda b, t: (0, 0, 0))
    tok = lambda w: pl.BlockSpec((1, tm, w), lambda b, t: (b, t, 0))

    ins = [x, mod, lw["g1"], lw["w1"], lw["dft64"], lw["gq"], lw["wq"]]
    specs = [tok(D_MODEL), pl.BlockSpec((1, N_MOD, D_MODEL), mod_idx),
             _const_spec((1, D_MODEL)), _const_spec((D_MODEL, W1_COLS)),
             _const_spec((F_WIDTH, 2 * F_WIDTH)), _const_spec((1, Q_LORA)),
             _const_spec((Q_LORA, MLA_HEADS * HEAD_PAD))]
    if rope:
        ins.append(lw["wqr"])
        specs.append(_const_spec((Q_LORA, MLA_HEADS * HEAD_PAD)))
    ins += [lw["gkv"], lw["wkt"], lw["wv"], lw["wkrt"], lw["wa"], lw["ba"]]
    specs += [_const_spec((1, KV_LORA)), _const_spec((MLA_HEADS * QK_NOPE, KV_LORA)),
              _const_spec((KV_LORA, MLA_WIDTH)), _const_spec((2 * QK_ROPE, D_MODEL)),
              _const_spec((LANES, 2 * GLA_QK)), _const_spec((1, 2 * GLA_QK))]
    if rope:
        ins += [tabs["cos_q"], tabs["sin_q"], tabs["cos_t"], tabs["sin_t"]]
        specs += [pl.BlockSpec((tm, HEAD_PAD), lambda b, t: (t, 0)),
                  pl.BlockSpec((tm, HEAD_PAD), lambda b, t: (t, 0)),
                  pl.BlockSpec((QK_ROPE, tm), lambda b, t: (0, t)),
                  pl.BlockSpec((QK_ROPE, tm), lambda b, t: (0, t))]

    def sds(w, dt):
        return jax.ShapeDtypeStruct((nb, seq, w), dt)

    out_shape = [sds(2 * F_WIDTH, BF16), sds(MLA_HEADS * HEAD_PAD, BF16),
                 jax.ShapeDtypeStruct((nb, MLA_HEADS, HEAD_PAD, seq), BF16),
                 sds(MLA_WIDTH, BF16), sds(2 * GLA_QK, F32), sds(GLA_WIDTH, F32),
                 sds(2 * GLA_QK, F32), sds(GLA_WIDTH, F32), sds(N_BRANCH * D_MODEL, BF16)]
    out_specs = [tok(2 * F_WIDTH), tok(MLA_HEADS * HEAD_PAD),
                 pl.BlockSpec((1, MLA_HEADS, HEAD_PAD, tm), lambda b, t: (b, 0, 0, t)),
                 tok(MLA_WIDTH), tok(2 * GLA_QK), tok(GLA_WIDTH), tok(2 * GLA_QK), tok(GLA_WIDTH),
                 tok(N_BRANCH * D_MODEL)]
    if emit_ctx:
        out_shape += [sds(KV_LORA, F32), sds(QK_ROPE, F32)]
        out_specs += [tok(KV_LORA), tok(QK_ROPE)]

    return pl.pallas_call(
        functools.partial(_inproj_kernel, rope=rope, emit_ctx=emit_ctx),
        grid=(nb, nt), in_specs=specs, out_specs=out_specs, out_shape=out_shape,
        compiler_params=_params(("parallel", "parallel")),
        name="in_proj_lat" if rope else "in_proj_ctx",
    )(*ins)


def _kvcache_kernel(ckv_ref, krt_ref, wkt_ref, wv_ref, kt_ref, v_ref):
    ckvb = ckv_ref[0].astype(BF16)
    knt = _dot_nt(wkt_ref[...], ckvb)
    v_ref[0] = _dot(ckvb, wv_ref[...]).astype(BF16)
    krt = krt_ref[0].astype(BF16)
    zero_pad = jnp.zeros((HEAD_PAD - QK_NOPE - QK_ROPE, krt.shape[1]), BF16)
    for hd in range(MLA_HEADS):
        kt_ref[0, hd, 0:QK_NOPE, :] = knt[hd * QK_NOPE:(hd + 1) * QK_NOPE].astype(BF16)
        kt_ref[0, hd, QK_NOPE:QK_NOPE + QK_ROPE, :] = krt
        kt_ref[0, hd, QK_NOPE + QK_ROPE:HEAD_PAD, :] = zero_pad


def _kvcache_call(ckv, krt, lw):
    nb, past, _ = ckv.shape
    return pl.pallas_call(
        _kvcache_kernel,
        grid=(nb,),
        in_specs=[pl.BlockSpec((1, past, KV_LORA), lambda b: (b, 0, 0)),
                  pl.BlockSpec((1, QK_ROPE, past), lambda b: (b, 0, 0)),
                  _const_spec((MLA_HEADS * QK_NOPE, KV_LORA)), _const_spec((KV_LORA, MLA_WIDTH))],
        out_specs=[pl.BlockSpec((1, MLA_HEADS, HEAD_PAD, past), lambda b: (b, 0, 0, 0)),
                   pl.BlockSpec((1, past, MLA_WIDTH), lambda b: (b, 0, 0))],
        out_shape=[jax.ShapeDtypeStruct((nb, MLA_HEADS, HEAD_PAD, past), BF16),
                   jax.ShapeDtypeStruct((nb, past, MLA_WIDTH), BF16)],
        compiler_params=_params(("parallel",)),
        name="kv_cache",
    )(ckv, krt, lw["wkt"], lw["wv"])


def _fourier_kernel(c_ref, ns_ref, fcs_ref, o_ref, *, norm):
    fc = fcs_ref[0, :, 0:F_WIDTH]
    fs = fcs_ref[0, :, F_WIDTH:2 * F_WIDTH]
    o_ref[0] = ((_dot(c_ref[...], fc) + _dot(ns_ref[...], fs)) * norm).astype(BF16)


def _fourier_call(fcs, dft_c, dft_ns, *, tr):
    nb, seq, _ = fcs.shape
    norm = 1.0 / math.sqrt(seq * F_GROUP_DIM)
    return pl.pallas_call(
        functools.partial(_fourier_kernel, norm=norm),
        grid=(seq // tr, nb),
        in_specs=[pl.BlockSpec((tr, seq), lambda i, b: (i, 0)),
                  pl.BlockSpec((tr, seq), lambda i, b: (i, 0)),
                  pl.BlockSpec((1, seq, 2 * F_WIDTH), lambda i, b: (b, 0, 0))],
        out_specs=pl.BlockSpec((1, tr, F_WIDTH), lambda i, b: (b, i, 0)),
        out_shape=jax.ShapeDtypeStruct((nb, seq, F_WIDTH), BF16),
        compiler_params=_params(("parallel", "parallel")),
        name="fourier",
    )(dft_c, dft_ns, fcs)


def _attn_kernel(*refs, has_cache, ck, ru):
    if has_cache:
        q_ref, kt_ref, v_ref, ktc_ref, vc_ref, o_ref, s_scr, p_scr = refs
    else:
        q_ref, kt_ref, v_ref, o_ref, s_scr, p_scr = refs
        ktc_ref = vc_ref = None

    chunks = []
    col = 0
    for k_r, v_r in ((ktc_ref, vc_ref), (kt_ref, v_ref)):
        if k_r is None:
            continue
        for off in range(0, k_r.shape[3], ck):
            w = min(ck, k_r.shape[3] - off)
            chunks.append((k_r, v_r, off, col, w))
            col += w

    def fold(t, x, op):
        for c in range(0, x.shape[1], LANES):
            piece = x[:, c:c + LANES]
            t = piece if t is None else op(t, piece)
        return t

    def scores(t, chunk, m_t):
        j, rows = units[t]
        k_r, _, off, c0, w = chunk
        s = _dot(q_ref[0, rows, j * HEAD_PAD:(j + 1) * HEAD_PAD], k_r[0, j, :, off:off + w])
        s_scr[t % ATTN_SLOTS, :, c0:c0 + w] = s
        return fold(m_t, s, jnp.maximum)

    def probs(t, chunk, m, l_t):
        _, _, _, c0, w = chunk
        p = jnp.exp2(s_scr[t % ATTN_SLOTS, :, c0:c0 + w] - m)
        p_scr[t % ATTN_SLOTS, :, c0:c0 + w] = p.astype(BF16)
        return fold(l_t, p, jnp.add)

    def weighted(t, chunk, acc):
        _, v_r, off, c0, w = chunk
        pv = _dot(p_scr[t % ATTN_SLOTS, :, c0:c0 + w], v_r[0, off:off + w, :])
        return pv if acc is None else acc + pv

    tq = q_ref.shape[1]
    units = [(j, slice(r0, r0 + ru)) for r0 in range(0, tq, ru) for j in range(HEADS_PER_STEP)]
    n_u = len(units)
    m_t, l_t, acc, m = ([None] * n_u for _ in range(4))
    lane = lax.broadcasted_iota(jnp.int32, (ru, HEADS_PER_STEP * V_DIM), 1)
    for t in range(n_u + 2):
        if 1 <= t <= n_u:
            m[t - 1] = jnp.max(m_t[t - 1], axis=-1, keepdims=True)
        for chunk in chunks:
            if t < n_u:
                m_t[t] = scores(t, chunk, m_t[t])
            if 1 <= t <= n_u:
                l_t[t - 1] = probs(t - 1, chunk, m[t - 1], l_t[t - 1])
            if 2 <= t:
                acc[t - 2] = weighted(t - 2, chunk, acc[t - 2])
        u = t - 2
        if u >= 0 and units[u][0] == HEADS_PER_STEP - 1:
            pair = [acc[v] / jnp.sum(l_t[v], axis=-1, keepdims=True) for v in (u - 1, u)]
            o_ref[0, units[u][1], :] = jnp.where(lane < V_DIM, pair[0], pair[1]).astype(BF16)


def _attn_call(q, kt, v, cache, *, tq, ck, ru):
    nb, seq, _ = q.shape
    has_cache = cache is not None
    n_keys = seq + (cache[0].shape[3] if has_cache else 0)
    pair_w = HEADS_PER_STEP * V_DIM
    ins = [q, kt, v]
    specs = [pl.BlockSpec((1, tq, HEADS_PER_STEP * HEAD_PAD), lambda b, p, i: (b, i, p)),
             pl.BlockSpec((1, HEADS_PER_STEP, HEAD_PAD, seq), lambda b, p, i: (b, p, 0, 0)),
             pl.BlockSpec((1, seq, pair_w), lambda b, p, i: (b, 0, p))]
    if has_cache:
        past = cache[0].shape[3]
        ins += list(cache)
        specs += [pl.BlockSpec((1, HEADS_PER_STEP, HEAD_PAD, past), lambda b, p, i: (b, p, 0, 0)),
                  pl.BlockSpec((1, past, pair_w), lambda b, p, i: (b, 0, p))]
    return pl.pallas_call(
        functools.partial(_attn_kernel, has_cache=has_cache, ck=ck, ru=ru),
        grid=(nb, MLA_HEADS // HEADS_PER_STEP, seq // tq),
        in_specs=specs,
        out_specs=pl.BlockSpec((1, tq, pair_w), lambda b, p, i: (b, i, p)),
        out_shape=jax.ShapeDtypeStruct((nb, seq, MLA_WIDTH), BF16),
        scratch_shapes=[pltpu.VMEM((ATTN_SLOTS, ru, n_keys), F32),
                        pltpu.VMEM((ATTN_SLOTS, ru, n_keys), BF16)],
        compiler_params=_params(("parallel", "parallel", "parallel")),
        name="attention_lat" if has_cache else "attention_ctx",
    )(*ins)


def _gla_tile(la, q, k, v, st_ref, cum_ref, tot_ref, e2_ref, mask_ref, *, reverse):
    cum = _dot_exact_lhs(cum_ref[...], la) * LOG2_E
    tot = _dot_exact_lhs(tot_ref[...], la) * LOG2_E
    qe = (q * jnp.exp2(cum)).astype(BF16)
    ke = (k * jnp.exp2(tot - cum)).astype(BF16)
    dec = jnp.exp2(tot)
    jrow = lax.broadcasted_iota(jnp.int32, (GLA_SUB, GLA_QK), 0)
    n_sub = GLA_TILE // GLA_SUB
    subs = [slice(n * GLA_SUB, (n + 1) * GLA_SUB) for n in range(n_sub)]

    upd = [_dot_tn(v[r].astype(BF16), ke[r]) * mask_ref[...] for r in subs]
    intra = []
    for r in subs:
        bq, qq, kk, vv = cum[r], q[r], k[r], v[r]
        rows = []
        for i in range(GLA_SUB):
            keep = (jrow >= i) if reverse else (jrow <= i)
            d = jnp.where(keep, bq[i:i + 1] - bq, GLA_MASKED_LOG2)
            rows.append(qq[i:i + 1] * kk * jnp.exp2(d))
        wmat = jnp.concatenate(rows, axis=0).astype(BF16)
        a = _dot(wmat, e2_ref[...])
        intra.append(jnp.sum(a.reshape(GLA_SUB, GLA_SUB, GLA_WIDTH) * vv[None], axis=1))

    st = st_ref[...]
    outs = [None] * n_sub
    for n in (range(n_sub - 1, -1, -1) if reverse else range(n_sub)):
        outs[n] = _dot_nt(qe[subs[n]], st.astype(BF16)) + intra[n]
        st = st * dec[n * GLA_SUB:n * GLA_SUB + 1] + upd[n]
    st_ref[...] = st
    return jnp.concatenate(outs, axis=0)


def _gla_kernel(qk_ref, gv_ref, la_ref, sg_ref, gn_ref, s0_ref, tril_ref, triu_ref, tot_ref, e2_ref,
                e3_ref, mask_ref, og_ref, sout_ref, of_scr, ob_scr, stf_scr, stb_scr):
    seq = qk_ref.shape[1]
    n_tiles = seq // GLA_TILE

    def tile_rows(t):
        return pl.ds(pl.multiple_of(t * GLA_TILE, GLA_TILE), GLA_TILE)

    def load(rows, lo):
        return (la_ref[0, rows, lo:lo + GLA_QK], qk_ref[0, rows, 0:GLA_QK],
                qk_ref[0, rows, GLA_QK:2 * GLA_QK], gv_ref[0, rows, :])

    stf_scr[...] = s0_ref[0, 0]
    stb_scr[...] = s0_ref[0, 1]

    def sweep(t, carry):
        rf = tile_rows(t)
        rb = tile_rows(n_tiles - 1 - t)
        of_scr[rf, :] = _gla_tile(*load(rf, 0), stf_scr, tril_ref, tot_ref, e2_ref, mask_ref, reverse=False)
        ob_scr[rb, :] = _gla_tile(*load(rb, GLA_QK), stb_scr, triu_ref, tot_ref, e2_ref, mask_ref, reverse=True)
        return carry

    lax.fori_loop(0, n_tiles, sweep, 0)
    sout_ref[0, 0] = stf_scr[...]
    sout_ref[0, 1] = stb_scr[...]

    def finish(t, carry):
        rows = tile_rows(t)
        o = of_scr[rows, :] + ob_scr[rows, :]
        o2 = o * o
        hi = o2.astype(BF16)
        lo = (o2 - hi.astype(F32)).astype(BF16)
        ms = _dot(hi, e3_ref[...]) + _dot(lo, e3_ref[...])
        og = o * lax.rsqrt(ms + NORM_EPS) * gn_ref[...] * sg_ref[0, rows, :]
        og_ref[0, rows, :] = og.astype(BF16)
        return carry

    lax.fori_loop(0, n_tiles, finish, 0)


def _gla_call(qk, gv, la, sg, gnorm, s0t, consts):
    nb, seq, _ = qk.shape
    tok = lambda w: pl.BlockSpec((1, seq, w), lambda b: (b, 0, 0))
    st_spec = pl.BlockSpec((1, 2, GLA_WIDTH, GLA_QK), lambda b: (b, 0, 0, 0))
    return pl.pallas_call(
        _gla_kernel,
        grid=(nb,),
        in_specs=[tok(2 * GLA_QK), tok(GLA_WIDTH), tok(2 * GLA_QK), tok(GLA_WIDTH),
                  _const_spec((1, GLA_WIDTH)), st_spec,
                  _const_spec((GLA_TILE, GLA_TILE)), _const_spec((GLA_TILE, GLA_TILE)),
                  _const_spec((GLA_TILE, GLA_TILE)), _const_spec((GLA_QK, GLA_WIDTH)),
                  _const_spec((GLA_WIDTH, GLA_WIDTH)), _const_spec((GLA_WIDTH, GLA_QK))],
        out_specs=[tok(GLA_WIDTH), st_spec],
        out_shape=[jax.ShapeDtypeStruct((nb, seq, GLA_WIDTH), BF16),
                   jax.ShapeDtypeStruct((nb, 2, GLA_WIDTH, GLA_QK), F32)],
        scratch_shapes=[pltpu.VMEM((seq, GLA_WIDTH), F32), pltpu.VMEM((seq, GLA_WIDTH), F32),
                        pltpu.VMEM((GLA_WIDTH, GLA_QK), F32), pltpu.VMEM((GLA_WIDTH, GLA_QK), F32)],
        compiler_params=_params(("parallel",)),
        name="gla",
    )(qk, gv, la, sg, gnorm, s0t, consts["tril"], consts["triu"], consts["tot"], consts["e2"],
      consts["e3"], consts["mask"])


def _tail_kernel(x_ref, f_ref, at_ref, og_ref, gate_ref, mod_ref, wof_ref, wom_ref, wog_ref, wout_ref,
                 g2_ref, w1_ref, w2_ref, gf_ref, o_ref, *, final, ff_chunk):
    mod = mod_ref[0]
    gate1, shift2, scale2, gate2 = mod[2:3], mod[3:4], mod[4:5], mod[5:6]
    ya = _dot(f_ref[0], wof_ref[...])
    yb = _dot(at_ref[0], wom_ref[...])
    yc = _dot(og_ref[0], wog_ref[...])
    merged = (gate_ref[0, :, 0:D_MODEL].astype(F32) * ya
              + gate_ref[0, :, D_MODEL:2 * D_MODEL].astype(F32) * yb
              + gate_ref[0, :, 2 * D_MODEL:3 * D_MODEL].astype(F32) * yc)
    x1 = x_ref[0] + gate1 * _dot(merged.astype(BF16), wout_ref[...])
    h2 = (_rms(x1) * g2_ref[...] * (1.0 + scale2) + shift2).astype(BF16)
    acc = None
    for c0 in range(0, D_FF, ff_chunk):
        u = jnp.maximum(_dot(h2, w1_ref[:, c0:c0 + ff_chunk]), 0.0)
        part = _dot((u * u).astype(BF16), w2_ref[c0:c0 + ff_chunk, :])
        acc = part if acc is None else acc + part
    x2 = x1 + gate2 * acc
    if final:
        x2 = _rms(x2) * gf_ref[...]
    o_ref[0] = x2


def _tail_call(x, f, attn, og, gates, mod, lw, final_g, *, final, tm):
    nb, seq, _ = x.shape
    mod_idx = (lambda b, t: (b, 0, 0)) if mod.shape[0] == nb else (lambda b, t: (0, 0, 0))
    tok = lambda w: pl.BlockSpec((1, tm, w), lambda b, t: (b, t, 0))
    return pl.pallas_call(
        functools.partial(_tail_kernel, final=final, ff_chunk=1024),
        grid=(nb, seq // tm),
        in_specs=[tok(D_MODEL), tok(F_WIDTH), tok(MLA_WIDTH), tok(GLA_WIDTH), tok(N_BRANCH * D_MODEL),
                  pl.BlockSpec((1, N_MOD, D_MODEL), mod_idx),
                  _const_spec((F_WIDTH, D_MODEL)), _const_spec((MLA_WIDTH, D_MODEL)),
                  _const_spec((GLA_WIDTH, D_MODEL)), _const_spec((D_MODEL, D_MODEL)),
                  _const_spec((1, D_MODEL)), _const_spec((D_MODEL, D_FF)), _const_spec((D_FF, D_MODEL)),
                  _const_spec((1, D_MODEL))],
        out_specs=tok(D_MODEL),
        out_shape=jax.ShapeDtypeStruct((nb, seq, D_MODEL), F32),
        compiler_params=_params(("parallel", "parallel")),
        name="tail",
    )(x, f, attn, og, gates, mod, lw["wof"], lw["wom"], lw["wog"], lw["wout"], lw["g2"], lw["wff1"],
      lw["wff2"], final_g)


def _rot_partner(w):
    q = QK_ROPE // 4
    return jnp.concatenate([-w[..., q:2 * q], w[..., 0:q], -w[..., 3 * q:4 * q], w[..., 2 * q:3 * q]], axis=-1)


def _rope_tables(seq):
    half = QK_ROPE // 2
    pos = jnp.arange(seq)
    row = (pos // GRID_W).astype(F32)
    col = (pos % GRID_W).astype(F32)
    inv = ROPE_BASE ** (-jnp.arange(0, half, 2, dtype=F32) / half)
    ang = jnp.concatenate([row[:, None] * inv, row[:, None] * inv, col[:, None] * inv, col[:, None] * inv], axis=1)
    cos, sin = jnp.cos(ang), jnp.sin(ang)
    pad = HEAD_PAD - QK_NOPE - QK_ROPE
    cos_q = jnp.concatenate([jnp.ones((seq, QK_NOPE), F32), cos, jnp.zeros((seq, pad), F32)], axis=1)
    sin_q = jnp.concatenate([jnp.zeros((seq, QK_NOPE), F32), sin, jnp.zeros((seq, pad), F32)], axis=1)
    return dict(cos_q=cos_q, sin_q=sin_q, cos_t=cos.T, sin_t=sin.T)


def _dft_tables(seq):
    split = math.gcd(seq, GRID_W)
    pos = jnp.arange(seq, dtype=jnp.int32)
    w = 2.0 * math.pi / seq

    def trig(k, axis):
        ang = ((k[:, None] * pos[None, :]) % seq).astype(F32) * w
        return jnp.expand_dims(jnp.cos(ang), axis), jnp.expand_dims(jnp.sin(ang), axis)

    ca, sa = trig(jnp.arange(seq // split, dtype=jnp.int32) * split, 1)
    cb, sb = trig(jnp.arange(split, dtype=jnp.int32), 0)
    cos = (ca * cb - sa * sb).reshape(seq, seq)
    neg_sin = (-(sa * cb + ca * sb)).reshape(seq, seq)
    return cos.astype(BF16), neg_sin.astype(BF16)


def _dft64():
    k = jnp.arange(F_GROUP_DIM, dtype=jnp.int32)
    ang = ((k[:, None] * k[None, :]) % F_GROUP_DIM).astype(F32) * (2.0 * math.pi / F_GROUP_DIM)
    eye = jnp.eye(F_GROUPS, dtype=F32)
    return jnp.concatenate([jnp.kron(eye, jnp.cos(ang)), jnp.kron(eye, jnp.sin(ang))], axis=1).astype(BF16)


def _gla_consts():
    idx = np.arange(GLA_TILE)
    same = (idx[:, None] // GLA_SUB) == (idx[None, :] // GLA_SUB)
    tril = same & (idx[None, :] <= idx[:, None])
    triu = same & (idx[None, :] >= idx[:, None])
    hq = np.arange(GLA_QK) // GLA_DK
    hv = np.arange(GLA_WIDTH) // GLA_DV
    e2 = hq[:, None] == hv[None, :]
    e3 = (hv[:, None] == hv[None, :]) / GLA_DV
    mask = hv[:, None] == hq[None, :]
    return dict(tril=jnp.asarray(tril, BF16), triu=jnp.asarray(triu, BF16), tot=jnp.asarray(same, BF16),
                e2=jnp.asarray(e2, BF16), e3=jnp.asarray(e3, BF16), mask=jnp.asarray(mask, F32))


def _layer_weights(l, w_in, norm1_g, mla_q_norm_g, mla_kv_norm_g, w_q_up, w_kv_up, gla_wa2_f, gla_ba_f,
                   gla_wa2_b, gla_ba_b, gla_norm_g, w_o_fourier, w_o_mla, w_o_gla, w_out, norm2_g, w_ff1,
                   w_ff2, dft64):
    wi = w_in[l]
    o = np.cumsum([0, F_WIDTH, Q_LORA, KV_LORA, QK_ROPE, GLA_QK, GLA_QK, GLA_WIDTH, GLA_WIDTH,
                   GLA_GATE_RANK, GLA_GATE_RANK, N_BRANCH * D_MODEL])
    zf, zqd, zkvd, zkr, zgq, zgk, zgv, zgr, zaf, zab, zgate = (wi[:, int(o[i]):int(o[i + 1])] for i in range(11))
    small = jnp.concatenate([zkr, jnp.zeros((D_MODEL, S_AF - QK_ROPE), F32), zaf, zab,
                             jnp.zeros((D_MODEL, LANES - S_AB - GLA_GATE_RANK), F32)], axis=1)
    w1 = jnp.concatenate([zf, zqd, zkvd, zgq, zgk, small, zgv, zgr, zgate], axis=1).astype(BF16)
    wkrt = jnp.concatenate([zkr, _rot_partner(zkr)], axis=1).T.astype(BF16)

    wq3 = w_q_up[l].reshape(Q_LORA, MLA_HEADS, QK_NOPE + QK_ROPE)
    pad = jnp.zeros((Q_LORA, MLA_HEADS, HEAD_PAD - QK_NOPE - QK_ROPE), F32)
    wq = jnp.concatenate([wq3, pad], axis=2).reshape(Q_LORA, -1).astype(BF16)
    wqr = jnp.concatenate([jnp.zeros((Q_LORA, MLA_HEADS, QK_NOPE), F32), _rot_partner(wq3[..., QK_NOPE:]), pad],
                          axis=2).reshape(Q_LORA, -1).astype(BF16)
    wkv3 = w_kv_up[l].reshape(KV_LORA, MLA_HEADS, QK_NOPE + V_DIM)
    wkt = wkv3[..., :QK_NOPE].reshape(KV_LORA, -1).T.astype(BF16)
    wv = wkv3[..., QK_NOPE:].reshape(KV_LORA, -1).astype(BF16)

    wa = jnp.zeros((LANES, 2 * GLA_QK), F32)
    wa = wa.at[S_AF:S_AF + GLA_GATE_RANK, 0:GLA_QK].set(gla_wa2_f[l])
    wa = wa.at[S_AB:S_AB + GLA_GATE_RANK, GLA_QK:].set(gla_wa2_b[l]).astype(BF16)
    ba = jnp.concatenate([gla_ba_f[l], gla_ba_b[l]])[None]
    return dict(
        g1=norm1_g[l][None], w1=w1, dft64=dft64, gq=mla_q_norm_g[l][None], wq=wq, wqr=wqr,
        gkv=mla_kv_norm_g[l][None], wkt=wkt, wv=wv, wkrt=wkrt, wa=wa, ba=ba, gn=gla_norm_g[l][None],
        wof=w_o_fourier[l].astype(BF16), wom=w_o_mla[l].astype(BF16), wog=w_o_gla[l].astype(BF16),
        wout=w_out[l].astype(BF16), g2=norm2_g[l][None], wff1=w_ff1[l].astype(BF16),
        wff2=w_ff2[l].astype(BF16))


def _state_to_blockdiag_t(s):
    nb = s.shape[0]
    eye = jnp.eye(GLA_HEADS, dtype=s.dtype)
    st = jnp.einsum("bxhde,hg->bxhegd", s, eye)
    return st.reshape(nb, 2, GLA_WIDTH, GLA_QK)


def _blockdiag_t_to_state(st):
    nb = st.shape[0]
    s6 = st.reshape(nb, 2, GLA_HEADS, GLA_DV, GLA_HEADS, GLA_DK)
    diag = jnp.stack([s6[:, :, h, :, h, :] for h in range(GLA_HEADS)], axis=2)
    return jnp.swapaxes(diag, -1, -2)


def _pick_tile(seq, target):
    t = min(seq, target)
    assert seq % t == 0 and t % GLA_SUB == 0, (seq, t)
    return t


def _mixer_and_tail(x, mod, lw, tabs, dft, gla_c, cache, s0t, final_g, *, rope, final):
    nb, seq, _ = x.shape
    tm = _pick_tile(seq, 256)
    outs = _inproj_call(x, mod, lw, tabs, rope=rope, emit_ctx=not rope, tm=tm)
    fcs, q, kt, v, qk, gv, la, sg, gates = outs[:9]
    f = _fourier_call(fcs, dft[0], dft[1], tr=_pick_tile(seq, 512))
    tq = _pick_tile(seq, 512)
    attn = _attn_call(q, kt, v, cache, tq=tq, ck=512, ru=min(tq, 128))
    og, st_out = _gla_call(qk, gv, la, sg, lw["gn"], s0t, gla_c)
    y = _tail_call(x, f, attn, og, gates, mod, lw, final_g, final=final, tm=tm)
    return y, outs[9:], st_out


def kernel(x_prompt, x_sample, c, cache_mla_ckv, cache_mla_krope, state_gla, c_ctx, w_mod, b_mod, norm1_g, w_in, mla_q_norm_g, mla_kv_norm_g, w_q_up, w_kv_up, gla_wa2_f, gla_ba_f, gla_wa2_b, gla_ba_b, gla_norm_g, w_o_fourier, w_o_mla, w_o_gla, w_out, norm2_g, w_ff1, w_ff2, final_norm_g):
    depth = w_mod.shape[0]
    n_ctx, seq_ctx, _ = x_prompt.shape
    n_lat, seq_lat, _ = x_sample.shape
    assert seq_ctx % GLA_TILE == 0 and seq_lat % GLA_TILE == 0 and seq_lat % GRID_W == 0
    assert 1 + n_lat <= MOD_ROWS

    cond = jnp.concatenate([c_ctx[None], c, jnp.zeros((MOD_ROWS - 1 - n_lat, D_MODEL), F32)], axis=0)
    mod_all = _mod_call(cond, w_mod, b_mod).reshape(depth, MOD_ROWS, N_MOD, D_MODEL)

    dft64 = _dft64()
    gla_c = _gla_consts()
    tabs = _rope_tables(seq_lat)
    dft_ctx = _dft_tables(seq_ctx)
    dft_lat = _dft_tables(seq_lat)
    final_g = final_norm_g[None]
    zero_state = jnp.zeros((n_ctx, 2, GLA_WIDTH, GLA_QK), F32)

    xp, xs = x_prompt, x_sample
    ckv_list, kr_list, gla_list = [], [], []
    for l in range(depth):
        lw = _layer_weights(l, w_in, norm1_g, mla_q_norm_g, mla_kv_norm_g, w_q_up, w_kv_up, gla_wa2_f,
                            gla_ba_f, gla_wa2_b, gla_ba_b, gla_norm_g, w_o_fourier, w_o_mla, w_o_gla,
                            w_out, norm2_g, w_ff1, w_ff2, dft64)
        final = l == depth - 1
        mod_ctx = mod_all[l, 0:1]
        mod_lat = mod_all[l, 1:1 + n_lat]

        xp, (ckv_c, kr_c), st_c = _mixer_and_tail(xp, mod_ctx, lw, None, dft_ctx, gla_c, None, zero_state,
                                                  final_g, rope=False, final=final)
        ckv_list.append(ckv_c)
        kr_list.append(kr_c)
        gla_list.append(_blockdiag_t_to_state(st_c))

        cache = _kvcache_call(cache_mla_ckv[:, l], jnp.swapaxes(cache_mla_krope[:, l], 1, 2), lw)
        s0t = _state_to_blockdiag_t(state_gla[:, l].astype(F32))
        xs, _, _ = _mixer_and_tail(xs, mod_lat, lw, tabs, dft_lat, gla_c, cache, s0t, final_g,
                                   rope=True, final=final)

    return (xp, xs, jnp.stack(ckv_list, axis=1), jnp.stack(kr_list, axis=1), jnp.stack(gla_list, axis=1))
```

```python
import functools
import math

import numpy as np
import jax
import jax.numpy as jnp
from jax import lax
from jax.experimental import pallas as pl
from jax.experimental.pallas import tpu as pltpu

F32 = jnp.float32
BF16 = jnp.bfloat16

D_MODEL = 1024
GRID_W = 64
F_GROUPS = 4
F_GROUP_DIM = 64
F_WIDTH = F_GROUPS * F_GROUP_DIM
MLA_HEADS = 8
QK_NOPE = 64
QK_ROPE = 32
V_DIM = 64
Q_LORA = 256
KV_LORA = 128
MLA_WIDTH = MLA_HEADS * V_DIM
GLA_HEADS = 4
GLA_DK = 32
GLA_DV = 64
GLA_WIDTH = GLA_HEADS * GLA_DV
GLA_QK = GLA_HEADS * GLA_DK
GLA_GATE_RANK = 16
GLA_TAU = 16.0
D_FF = 4 * D_MODEL
N_BRANCH = 3
ROPE_BASE = 10000.0
NORM_EPS = 1e-6
N_MOD = 6
LOG2_E = math.log2(math.e)

LANES = 128
VMEM_LIMIT_BYTES = 56 * 1024 * 1024

HEAD_PAD = LANES
ATTN_SLOTS = 3
GLA_SUB = 16
GLA_TILE = 128
GLA_PAIR_ROWS = GLA_SUB * (GLA_SUB // 2) + (GLA_SUB // 2) ** 2
MOD_ROWS = 16
GLA_MASKED_LOG2 = -1e30

C_F = 0
C_QD = C_F + F_WIDTH
C_KVD = C_QD + Q_LORA
C_GQ = C_KVD + KV_LORA
C_GK = C_GQ + GLA_QK
C_SMALL = C_GK + GLA_QK
C_GV = C_SMALL + LANES
C_GR = C_GV + GLA_WIDTH
C_GATE = C_GR + GLA_WIDTH
W1_COLS = C_GATE + N_BRANCH * D_MODEL
S_KR = 0
S_AF = 64
S_AB = 80


def _dot(a, b):
    return jnp.dot(a, b, preferred_element_type=F32)


def _dot_nt(a, b):
    return lax.dot_general(a, b, (((1,), (1,)), ((), ())), preferred_element_type=F32)


def _dot_tn(a, b):
    return lax.dot_general(a, b, (((0,), (0,)), ((), ())), preferred_element_type=F32)


def _split3(x):
    hi = x.astype(BF16)
    r1 = x - hi.astype(F32)
    mid = r1.astype(BF16)
    lo = (r1 - mid.astype(F32)).astype(BF16)
    return hi, mid, lo


def _dot_exact_lhs(m, x):
    hi, mid, lo = _split3(x)
    return _dot(m, hi) + _dot(m, mid) + _dot(m, lo)


def _rms(x):
    return x * lax.rsqrt(jnp.mean(x * x, axis=-1, keepdims=True) + NORM_EPS)


def _sigmoid(x):
    return 1.0 / (1.0 + jnp.exp(-x))


def _const_spec(shape):
    return pl.BlockSpec(shape, lambda *_: (0,) * len(shape))


def _params(sem):
    return pltpu.CompilerParams(dimension_semantics=sem, vmem_limit_bytes=VMEM_LIMIT_BYTES)


def _mod_kernel(c_ref, w_ref, b_ref, o_ref):
    c = c_ref[...]
    s = (c * _sigmoid(c)).astype(BF16)
    o_ref[0] = _dot(s, w_ref[0].astype(BF16)) + b_ref[0]


def _mod_call(cond, w_mod, b_mod):
    depth = w_mod.shape[0]
    n_col = w_mod.shape[2] // D_MODEL
    return pl.pallas_call(
        _mod_kernel,
        grid=(depth, n_col),
        in_specs=[
            pl.BlockSpec((MOD_ROWS, D_MODEL), lambda l, j: (0, 0)),
            pl.BlockSpec((1, D_MODEL, D_MODEL), lambda l, j: (l, 0, j)),
            pl.BlockSpec((1, 1, D_MODEL), lambda l, j: (l, 0, j)),
        ],
        out_specs=pl.BlockSpec((1, MOD_ROWS, D_MODEL), lambda l, j: (l, 0, j)),
        out_shape=jax.ShapeDtypeStruct((depth, MOD_ROWS, w_mod.shape[2]), F32),
        compiler_params=_params(("arbitrary", "arbitrary")),
        name="mod_proj",
    )(cond, w_mod, b_mod.reshape(depth, 1, -1))


def _inproj_kernel(*refs, rope, emit_ctx):
    it = iter(refs)
    x_ref, mod_ref, g1_ref, w1_ref, dft_ref, gq_ref, wq_ref = (next(it) for _ in range(7))
    wqr_ref = next(it) if rope else None
    gkv_ref, wkt_ref, wv_ref, wkrt_ref, wa_ref, ba_ref = (next(it) for _ in range(6))
    if rope:
        cq_ref, sq_ref, ct_ref, st_ref = (next(it) for _ in range(4))
    fcs_ref, q_ref, kt_ref, v_ref, qk_ref, gv_ref, la_ref, sg_ref, gate_ref = (next(it) for _ in range(9))
    if emit_ctx:
        ckv_ref, kr_ref = next(it), next(it)

    x = x_ref[0]
    mod = mod_ref[0]
    shift1, scale1 = mod[0:1], mod[1:2]
    h = _rms(x) * g1_ref[...] * (1.0 + scale1) + shift1
    hb = h.astype(BF16)

    def branch_gate(br):
        lo = C_GATE + br * D_MODEL
        zg = _dot(hb, w1_ref[:, lo:lo + D_MODEL])
        gate_ref[0, :, br * D_MODEL:(br + 1) * D_MODEL] = _sigmoid(zg).astype(BF16)

    z_qd = _dot(hb, w1_ref[:, C_QD:C_QD + Q_LORA])
    z_kg = _dot(hb, w1_ref[:, C_KVD:C_KVD + KV_LORA + GLA_QK])
    z_ks = _dot(hb, w1_ref[:, C_GK:C_GK + GLA_QK + LANES])
    krt2 = _dot_nt(wkrt_ref[...], hb)
    branch_gate(0)

    cq = (_rms(z_qd) * gq_ref[...]).astype(BF16)
    qa = _dot(cq, wq_ref[...])
    scale = (QK_NOPE + QK_ROPE) ** -0.5 * LOG2_E
    if rope:
        qb = _dot(cq, wqr_ref[...])
        cos_q, sin_q = cq_ref[...], sq_ref[...]
        for hd in range(MLA_HEADS):
            sl = slice(hd * HEAD_PAD, (hd + 1) * HEAD_PAD)
            q_ref[0, :, sl] = ((qa[:, sl] * cos_q + qb[:, sl] * sin_q) * scale).astype(BF16)
    else:
        q_ref[0] = (qa * scale).astype(BF16)

    ckv = _rms(z_kg[:, 0:KV_LORA]) * gkv_ref[...]
    ckvb = ckv.astype(BF16)
    knt = _dot_nt(wkt_ref[...], ckvb)
    v_ref[0] = _dot(ckvb, wv_ref[...]).astype(BF16)
    krt = krt2[0:QK_ROPE]
    if rope:
        krt = krt * ct_ref[...] + krt2[QK_ROPE:2 * QK_ROPE] * st_ref[...]
    krt = krt.astype(BF16)
    zero_pad = jnp.zeros((HEAD_PAD - QK_NOPE - QK_ROPE, krt.shape[1]), BF16)
    for hd in range(MLA_HEADS):
        kt_ref[0, hd, 0:QK_NOPE, :] = knt[hd * QK_NOPE:(hd + 1) * QK_NOPE].astype(BF16)
        kt_ref[0, hd, QK_NOPE:QK_NOPE + QK_ROPE, :] = krt
        kt_ref[0, hd, QK_NOPE + QK_ROPE:HEAD_PAD, :] = zero_pad
    small = z_ks[:, GLA_QK:GLA_QK + LANES]
    if emit_ctx:
        ckv_ref[0] = ckv
        kr_ref[0] = small[:, S_KR:S_KR + QK_ROPE]
    branch_gate(1)

    zf = _dot(hb, w1_ref[:, C_F:C_F + F_WIDTH])

    qk_ref[0, :, 0:GLA_QK] = z_kg[:, KV_LORA:KV_LORA + GLA_QK] * (GLA_DK ** -0.5)
    qk_ref[0, :, GLA_QK:2 * GLA_QK] = z_ks[:, 0:GLA_QK]
    gv_ref[0] = _dot(hb, w1_ref[:, C_GV:C_GV + GLA_WIDTH])
    a_pre = _dot(small.astype(BF16), wa_ref[...]) + ba_ref[...]
    log_sig = jnp.minimum(a_pre, 0.0) - jnp.log(1.0 + jnp.exp(-jnp.abs(a_pre)))
    la_ref[0] = log_sig * (1.0 / GLA_TAU)
    zgr = _dot(hb, w1_ref[:, C_GR:C_GR + GLA_WIDTH])
    sg_ref[0] = zgr * _sigmoid(zgr)
    fcs_ref[0] = _dot(zf.astype(BF16), dft_ref[...]).astype(BF16)
    branch_gate(2)


def _inproj_call(x, mod, lw, tabs, *, rope, emit_ctx, tm):
    nb, seq, _ = x.shape
    nt = seq // tm
    mod_idx = (lambda b, t: (b, 0, 0)) if mod.shape[0] == nb else (lambda b, t: (0, 0, 0))
    tok = lambda w: pl.BlockSpec((1, tm, w), lambda b, t: (b, t, 0))

    ins = [x, mod, lw["g1"], lw["w1"], lw["dft64"], lw["gq"], lw["wq"]]
    specs = [tok(D_MODEL), pl.BlockSpec((1, N_MOD, D_MODEL), mod_idx),
             _const_spec((1, D_MODEL)), _const_spec((D_MODEL, W1_COLS)),
             _const_spec((F_WIDTH, 2 * F_WIDTH)), _const_spec((1, Q_LORA)),
             _const_spec((Q_LORA, MLA_HEADS * HEAD_PAD))]
    if rope:
        ins.append(lw["wqr"])
        specs.append(_const_spec((Q_LORA, MLA_HEADS * HEAD_PAD)))
    ins += [lw["gkv"], lw["wkt"], lw["wv"], lw["wkrt"], lw["wa"], lw["ba"]]
    specs += [_const_spec((1, KV_LORA)), _const_spec((MLA_HEADS * QK_NOPE, KV_LORA)),
              _const_spec((KV_LORA, MLA_WIDTH)), _const_spec((2 * QK_ROPE, D_MODEL)),
              _const_spec((LANES, 2 * GLA_QK)), _const_spec((1, 2 * GLA_QK))]
    if rope:
        ins += [tabs["cos_q"], tabs["sin_q"], tabs["cos_t"], tabs["sin_t"]]
        specs += [pl.BlockSpec((tm, HEAD_PAD), lambda b, t: (t, 0)),
                  pl.BlockSpec((tm, HEAD_PAD), lambda b, t: (t, 0)),
                  pl.BlockSpec((QK_ROPE, tm), lambda b, t: (0, t)),
                  pl.BlockSpec((QK_ROPE, tm), lambda b, t: (0, t))]

    def sds(w, dt):
        return jax.ShapeDtypeStruct((nb, seq, w), dt)

    out_shape = [sds(2 * F_WIDTH, BF16), sds(MLA_HEADS * HEAD_PAD, BF16),
                 jax.ShapeDtypeStruct((nb, MLA_HEADS, HEAD_PAD, seq), BF16),
                 sds(MLA_WIDTH, BF16), sds(2 * GLA_QK, F32), sds(GLA_WIDTH, F32),
                 sds(2 * GLA_QK, F32), sds(GLA_WIDTH, F32), sds(N_BRANCH * D_MODEL, BF16)]
    out_specs = [tok(2 * F_WIDTH), tok(MLA_HEADS * HEAD_PAD),
                 pl.BlockSpec((1, MLA_HEADS, HEAD_PAD, tm), lambda b, t: (b, 0, 0, t)),
                 tok(MLA_WIDTH), tok(2 * GLA_QK), tok(GLA_WIDTH), tok(2 * GLA_QK), tok(GLA_WIDTH),
                 tok(N_BRANCH * D_MODEL)]
    if emit_ctx:
        out_shape += [sds(KV_LORA, F32), sds(QK_ROPE, F32)]
        out_specs += [tok(KV_LORA), tok(QK_ROPE)]

    return pl.pallas_call(
        functools.partial(_inproj_kernel, rope=rope, emit_ctx=emit_ctx),
        grid=(nb, nt), in_specs=specs, out_specs=out_specs, out_shape=out_shape,
        compiler_params=_params(("parallel", "parallel")),
        name="in_proj_lat" if rope else "in_proj_ctx",
    )(*ins)


def _kvcache_kernel(ckv_ref, krt_ref, wkt_ref, wv_ref, kt_ref, v_ref):
    ckvb = ckv_ref[0].astype(BF16)
    knt = _dot_nt(wkt_ref[...], ckvb)
    v_ref[0] = _dot(ckvb, wv_ref[...]).astype(BF16)
    krt = krt_ref[0].astype(BF16)
    zero_pad = jnp.zeros((HEAD_PAD - QK_NOPE - QK_ROPE, krt.shape[1]), BF16)
    for hd in range(MLA_HEADS):
        kt_ref[0, hd, 0:QK_NOPE, :] = knt[hd * QK_NOPE:(hd + 1) * QK_NOPE].astype(BF16)
        kt_ref[0, hd, QK_NOPE:QK_NOPE + QK_ROPE, :] = krt
        kt_ref[0, hd, QK_NOPE + QK_ROPE:HEAD_PAD, :] = zero_pad


def _kvcache_call(ckv, krt, lw):
    nb, past, _ = ckv.shape
    return pl.pallas_call(
        _kvcache_kernel,
        grid=(nb,),
        in_specs=[pl.BlockSpec((1, past, KV_LORA), lambda b: (b, 0, 0)),
                  pl.BlockSpec((1, QK_ROPE, past), lambda b: (b, 0, 0)),
                  _const_spec((MLA_HEADS * QK_NOPE, KV_LORA)), _const_spec((KV_LORA, MLA_WIDTH))],
        out_specs=[pl.BlockSpec((1, MLA_HEADS, HEAD_PAD, past), lambda b: (b, 0, 0, 0)),
                   pl.BlockSpec((1, past, MLA_WIDTH), lambda b: (b, 0, 0))],
        out_shape=[jax.ShapeDtypeStruct((nb, MLA_HEADS, HEAD_PAD, past), BF16),
                   jax.ShapeDtypeStruct((nb, past, MLA_WIDTH), BF16)],
        compiler_params=_params(("parallel",)),
        name="kv_cache",
    )(ckv, krt, lw["wkt"], lw["wv"])


def _fourier_kernel(c_ref, ns_ref, fcs_ref, o_ref, *, norm):
    fc = fcs_ref[0, :, 0:F_WIDTH]
    fs = fcs_ref[0, :, F_WIDTH:2 * F_WIDTH]
    o_ref[0] = ((_dot(c_ref[...], fc) + _dot(ns_ref[...], fs)) * norm).astype(BF16)


def _fourier_call(fcs, dft_c, dft_ns, *, tr):
    nb, seq, _ = fcs.shape
    norm = 1.0 / math.sqrt(seq * F_GROUP_DIM)
    return pl.pallas_call(
        functools.partial(_fourier_kernel, norm=norm),
        grid=(seq // tr, nb),
        in_specs=[pl.BlockSpec((tr, seq), lambda i, b: (i, 0)),
                  pl.BlockSpec((tr, seq), lambda i, b: (i, 0)),
                  pl.BlockSpec((1, seq, 2 * F_WIDTH), lambda i, b: (b, 0, 0))],
        out_specs=pl.BlockSpec((1, tr, F_WIDTH), lambda i, b: (b, i, 0)),
        out_shape=jax.ShapeDtypeStruct((nb, seq, F_WIDTH), BF16),
        compiler_params=_params(("parallel", "parallel")),
        name="fourier",
    )(dft_c, dft_ns, fcs)


def _attn_kernel(*refs, has_cache, ck, ru, hp):
    if has_cache:
        q_ref, kt_ref, v_ref, ktc_ref, vc_ref, o_ref, s_scr, p_scr = refs
    else:
        q_ref, kt_ref, v_ref, o_ref, s_scr, p_scr = refs
        ktc_ref = vc_ref = None

    chunks = []
    col = 0
    for k_r, v_r in ((ktc_ref, vc_ref), (kt_ref, v_ref)):
        if k_r is None:
            continue
        for off in range(0, k_r.shape[3], ck):
            w = min(ck, k_r.shape[3] - off)
            chunks.append((k_r, v_r, off, col, w))
            col += w

    def fold(t, x, op):
        for c in range(0, x.shape[1], LANES):
            piece = x[:, c:c + LANES]
            t = piece if t is None else op(t, piece)
        return t

    def scores(t, chunk, m_t):
        j, rows = units[t]
        k_r, _, off, c0, w = chunk
        s = _dot(q_ref[0, rows, j * HEAD_PAD:(j + 1) * HEAD_PAD], k_r[0, j, :, off:off + w])
        s_scr[t % ATTN_SLOTS, :, c0:c0 + w] = s
        return fold(m_t, s, jnp.maximum)

    def probs(t, chunk, m, l_t):
        _, _, _, c0, w = chunk
        p = jnp.exp2(s_scr[t % ATTN_SLOTS, :, c0:c0 + w] - m)
        p_scr[t % ATTN_SLOTS, :, c0:c0 + w] = p.astype(BF16)
        return fold(l_t, p, jnp.add)

    def weighted(t, chunk, acc):
        _, v_r, off, c0, w = chunk
        pair = units[t][0] // 2
        pv = _dot(p_scr[t % ATTN_SLOTS, :, c0:c0 + w], v_r[0, off:off + w, pair * LANES:(pair + 1) * LANES])
        return pv if acc is None else acc + pv

    tq = q_ref.shape[1]
    units = [(j, slice(r0, r0 + ru)) for r0 in range(0, tq, ru) for j in range(hp)]
    n_u = len(units)
    m_t, l_t, acc, m = ([None] * n_u for _ in range(4))
    lane = lax.broadcasted_iota(jnp.int32, (ru, LANES), 1)
    for t in range(n_u + 2):
        if 1 <= t <= n_u:
            m[t - 1] = jnp.max(m_t[t - 1], axis=-1, keepdims=True)
        for chunk in chunks:
            if t < n_u:
                m_t[t] = scores(t, chunk, m_t[t])
            if 1 <= t <= n_u:
                l_t[t - 1] = probs(t - 1, chunk, m[t - 1], l_t[t - 1])
            if 2 <= t:
                acc[t - 2] = weighted(t - 2, chunk, acc[t - 2])
        u = t - 2
        if u >= 0 and units[u][0] % 2 == 1:
            j, rows = units[u]
            even, odd = (acc[v] / jnp.sum(l_t[v], axis=-1, keepdims=True) for v in (u - 1, u))
            o_ref[0, rows, (j // 2) * LANES:(j // 2 + 1) * LANES] = jnp.where(lane < V_DIM, even, odd).astype(BF16)


def _attn_call(q, kt, v, cache, *, tq, ck, ru, hp):
    nb, seq, _ = q.shape
    has_cache = cache is not None
    n_keys = seq + (cache[0].shape[3] if has_cache else 0)
    assert hp % 2 == 0 and MLA_HEADS % hp == 0 and 2 * V_DIM == LANES
    val_w = hp * V_DIM
    ins = [q, kt, v]
    specs = [pl.BlockSpec((1, tq, hp * HEAD_PAD), lambda b, p, i: (b, i, p)),
             pl.BlockSpec((1, hp, HEAD_PAD, seq), lambda b, p, i: (b, p, 0, 0)),
             pl.BlockSpec((1, seq, val_w), lambda b, p, i: (b, 0, p))]
    if has_cache:
        past = cache[0].shape[3]
        ins += list(cache)
        specs += [pl.BlockSpec((1, hp, HEAD_PAD, past), lambda b, p, i: (b, p, 0, 0)),
                  pl.BlockSpec((1, past, val_w), lambda b, p, i: (b, 0, p))]
    return pl.pallas_call(
        functools.partial(_attn_kernel, has_cache=has_cache, ck=ck, ru=ru, hp=hp),
        grid=(nb, MLA_HEADS // hp, seq // tq),
        in_specs=specs,
        out_specs=pl.BlockSpec((1, tq, val_w), lambda b, p, i: (b, i, p)),
        out_shape=jax.ShapeDtypeStruct((nb, seq, MLA_WIDTH), BF16),
        scratch_shapes=[pltpu.VMEM((ATTN_SLOTS, ru, n_keys), F32),
                        pltpu.VMEM((ATTN_SLOTS, ru, n_keys), BF16)],
        compiler_params=_params(("parallel", "parallel", "parallel")),
        name="attention_lat" if has_cache else "attention_ctx",
    )(*ins)


def _gla_tile(la, q, k, v, st_ref, cum_ref, sel_ref, tot_ref, e2_ref, mask_ref, *, reverse):
    cum = _dot_exact_lhs(cum_ref[...], la) * LOG2_E
    tot = _dot_exact_lhs(tot_ref[...], la) * LOG2_E
    yield None

    qe = (q * jnp.exp2(cum)).astype(BF16)
    ke = (k * jnp.exp2(tot - cum)).astype(BF16)
    dec = jnp.exp2(tot)
    jrow = lax.broadcasted_iota(jnp.int32, (GLA_SUB, GLA_QK), 0)
    n_sub = GLA_TILE // GLA_SUB
    subs = [slice(n * GLA_SUB, (n + 1) * GLA_SUB) for n in range(n_sub)]

    upd = [_dot_tn(v[r].astype(BF16), ke[r]) * mask_ref[...] for r in subs]
    yield None

    half = GLA_SUB // 2
    near = slice(half, GLA_SUB) if reverse else slice(0, half)
    far = slice(0, half) if reverse else slice(half, GLA_SUB)
    n_full = GLA_SUB * half
    spread = []
    for r in subs:
        bq, qq, kk = cum[r], q[r], k[r]
        full, part = [], []
        for i in range(GLA_SUB):
            i_near = (i >= half) if reverse else (i < half)
            own = near if i_near else far
            keep = (jrow[own] >= i) if reverse else (jrow[own] <= i)
            d = jnp.where(keep, bq[i:i + 1] - bq[own], GLA_MASKED_LOG2)
            w_own = qq[i:i + 1] * kk[own] * jnp.exp2(d)
            if i_near:
                full.append(w_own)
            else:
                part.append(w_own)
                full.append(qq[i:i + 1] * kk[near] * jnp.exp2(bq[i:i + 1] - bq[near]))
        wmat = jnp.concatenate(full + part, axis=0).astype(BF16)
        spread.append(_dot(wmat, e2_ref[...]))
    yield None

    intra = []
    for r, a in zip(subs, spread):
        vv = v[r]
        av = jnp.concatenate([
            (a[:n_full].reshape(GLA_SUB, half, GLA_WIDTH) * vv[near][None]).reshape(n_full, GLA_WIDTH),
            (a[n_full:].reshape(half, half, GLA_WIDTH) * vv[far][None]).reshape(half * half, GLA_WIDTH)], axis=0)
        intra.append(_dot(sel_ref[...], av.astype(BF16)))
    yield None

    st = st_ref[...]
    outs = [None] * n_sub
    for n in (range(n_sub - 1, -1, -1) if reverse else range(n_sub)):
        outs[n] = _dot_nt(qe[subs[n]], st.astype(BF16)) + intra[n]
        st = st * dec[n * GLA_SUB:n * GLA_SUB + 1] + upd[n]
        yield None
    st_ref[...] = st
    yield jnp.concatenate(outs, axis=0)


def _alternate(*stage_generators):
    last = [None] * len(stage_generators)
    live = list(range(len(stage_generators)))
    while live:
        for g in list(live):
            try:
                last[g] = next(stage_generators[g])
            except StopIteration:
                live.remove(g)
    return last


def _gla_kernel(qk_ref, gv_ref, la_ref, sg_ref, gn_ref, s0_ref, tril_ref, triu_ref, self_ref, selb_ref,
                tot_ref, e2_ref, e3_ref, mask_ref, og_ref, sout_ref, of_scr, ob_scr, stf_scr, stb_scr):
    seq = qk_ref.shape[1]
    n_tiles = seq // GLA_TILE

    def tile_rows(t):
        return pl.ds(pl.multiple_of(t * GLA_TILE, GLA_TILE), GLA_TILE)

    def load(rows, lo):
        return (la_ref[0, rows, lo:lo + GLA_QK], qk_ref[0, rows, 0:GLA_QK],
                qk_ref[0, rows, GLA_QK:2 * GLA_QK], gv_ref[0, rows, :])

    stf_scr[...] = s0_ref[0, 0]
    stb_scr[...] = s0_ref[0, 1]

    def sweep(t, carry):
        rf = tile_rows(t)
        rb = tile_rows(n_tiles - 1 - t)
        shared = (tot_ref, e2_ref, mask_ref)
        of_scr[rf, :], ob_scr[rb, :] = _alternate(
            _gla_tile(*load(rf, 0), stf_scr, tril_ref, self_ref, *shared, reverse=False),
            _gla_tile(*load(rb, GLA_QK), stb_scr, triu_ref, selb_ref, *shared, reverse=True))
        return carry

    lax.fori_loop(0, n_tiles, sweep, 0)
    sout_ref[0, 0] = stf_scr[...]
    sout_ref[0, 1] = stb_scr[...]

    def finish(t, carry):
        rows = tile_rows(t)
        o = of_scr[rows, :] + ob_scr[rows, :]
        o2 = o * o
        hi = o2.astype(BF16)
        lo = (o2 - hi.astype(F32)).astype(BF16)
        ms = _dot(hi, e3_ref[...]) + _dot(lo, e3_ref[...])
        og = o * lax.rsqrt(ms + NORM_EPS) * gn_ref[...] * sg_ref[0, rows, :]
        og_ref[0, rows, :] = og.astype(BF16)
        return carry

    lax.fori_loop(0, n_tiles, finish, 0)


def _gla_call(qk, gv, la, sg, gnorm, s0t, consts):
    nb, seq, _ = qk.shape
    tok = lambda w: pl.BlockSpec((1, seq, w), lambda b: (b, 0, 0))
    st_spec = pl.BlockSpec((1, 2, GLA_WIDTH, GLA_QK), lambda b: (b, 0, 0, 0))
    return pl.pallas_call(
        _gla_kernel,
        grid=(nb,),
        in_specs=[tok(2 * GLA_QK), tok(GLA_WIDTH), tok(2 * GLA_QK), tok(GLA_WIDTH),
                  _const_spec((1, GLA_WIDTH)), st_spec,
                  _const_spec((GLA_TILE, GLA_TILE)), _const_spec((GLA_TILE, GLA_TILE)),
                  _const_spec((GLA_SUB, GLA_PAIR_ROWS)), _const_spec((GLA_SUB, GLA_PAIR_ROWS)),
                  _const_spec((GLA_TILE, GLA_TILE)), _const_spec((GLA_QK, GLA_WIDTH)),
                  _const_spec((GLA_WIDTH, GLA_WIDTH)), _const_spec((GLA_WIDTH, GLA_QK))],
        out_specs=[tok(GLA_WIDTH), st_spec],
        out_shape=[jax.ShapeDtypeStruct((nb, seq, GLA_WIDTH), BF16),
                   jax.ShapeDtypeStruct((nb, 2, GLA_WIDTH, GLA_QK), F32)],
        scratch_shapes=[pltpu.VMEM((seq, GLA_WIDTH), F32), pltpu.VMEM((seq, GLA_WIDTH), F32),
                        pltpu.VMEM((GLA_WIDTH, GLA_QK), F32), pltpu.VMEM((GLA_WIDTH, GLA_QK), F32)],
        compiler_params=_params(("parallel",)),
        name="gla",
    )(qk, gv, la, sg, gnorm, s0t, consts["tril"], consts["triu"], consts["sel_f"], consts["sel_b"],
      consts["tot"], consts["e2"], consts["e3"], consts["mask"])


def _tail_kernel(x_ref, f_ref, at_ref, og_ref, gate_ref, mod_ref, wof_ref, wom_ref, wog_ref, wout_ref,
                 g2_ref, w1_ref, w2_ref, gf_ref, o_ref, *, final, ff_chunk):
    mod = mod_ref[0]
    gate1, shift2, scale2, gate2 = mod[2:3], mod[3:4], mod[4:5], mod[5:6]
    ya = _dot(f_ref[0], wof_ref[...])
    yb = _dot(at_ref[0], wom_ref[...])
    yc = _dot(og_ref[0], wog_ref[...])
    merged = (gate_ref[0, :, 0:D_MODEL].astype(F32) * ya
              + gate_ref[0, :, D_MODEL:2 * D_MODEL].astype(F32) * yb
              + gate_ref[0, :, 2 * D_MODEL:3 * D_MODEL].astype(F32) * yc)
    x1 = x_ref[0] + gate1 * _dot(merged.astype(BF16), wout_ref[...])
    h2 = (_rms(x1) * g2_ref[...] * (1.0 + scale2) + shift2).astype(BF16)
    acc = None
    for c0 in range(0, D_FF, ff_chunk):
        u = jnp.maximum(_dot(h2, w1_ref[:, c0:c0 + ff_chunk]), 0.0)
        part = _dot((u * u).astype(BF16), w2_ref[c0:c0 + ff_chunk, :])
        acc = part if acc is None else acc + part
    x2 = x1 + gate2 * acc
    if final:
        x2 = _rms(x2) * gf_ref[...]
    o_ref[0] = x2


def _tail_call(x, f, attn, og, gates, mod, lw, final_g, *, final, tm):
    nb, seq, _ = x.shape
    mod_idx = (lambda b, t: (b, 0, 0)) if mod.shape[0] == nb else (lambda b, t: (0, 0, 0))
    tok = lambda w: pl.BlockSpec((1, tm, w), lambda b, t: (b, t, 0))
    return pl.pallas_call(
        functools.partial(_tail_kernel, final=final, ff_chunk=1024),
        grid=(nb, seq // tm),
        in_specs=[tok(D_MODEL), tok(F_WIDTH), tok(MLA_WIDTH), tok(GLA_WIDTH), tok(N_BRANCH * D_MODEL),
                  pl.BlockSpec((1, N_MOD, D_MODEL), mod_idx),
                  _const_spec((F_WIDTH, D_MODEL)), _const_spec((MLA_WIDTH, D_MODEL)),
                  _const_spec((GLA_WIDTH, D_MODEL)), _const_spec((D_MODEL, D_MODEL)),
                  _const_spec((1, D_MODEL)), _const_spec((D_MODEL, D_FF)), _const_spec((D_FF, D_MODEL)),
                  _const_spec((1, D_MODEL))],
        out_specs=tok(D_MODEL),
        out_shape=jax.ShapeDtypeStruct((nb, seq, D_MODEL), F32),
        compiler_params=_params(("parallel", "parallel")),
        name="tail",
    )(x, f, attn, og, gates, mod, lw["wof"], lw["wom"], lw["wog"], lw["wout"], lw["g2"], lw["wff1"],
      lw["wff2"], final_g)


def _rot_partner(w):
    q = QK_ROPE // 4
    return jnp.concatenate([-w[..., q:2 * q], w[..., 0:q], -w[..., 3 * q:4 * q], w[..., 2 * q:3 * q]], axis=-1)


def _rope_tables(seq):
    half = QK_ROPE // 2
    pos = jnp.arange(seq)
    row = (pos // GRID_W).astype(F32)
    col = (pos % GRID_W).astype(F32)
    inv = ROPE_BASE ** (-jnp.arange(0, half, 2, dtype=F32) / half)
    ang = jnp.concatenate([row[:, None] * inv, row[:, None] * inv, col[:, None] * inv, col[:, None] * inv], axis=1)
    cos, sin = jnp.cos(ang), jnp.sin(ang)
    pad = HEAD_PAD - QK_NOPE - QK_ROPE
    cos_q = jnp.concatenate([jnp.ones((seq, QK_NOPE), F32), cos, jnp.zeros((seq, pad), F32)], axis=1)
    sin_q = jnp.concatenate([jnp.zeros((seq, QK_NOPE), F32), sin, jnp.zeros((seq, pad), F32)], axis=1)
    return dict(cos_q=cos_q, sin_q=sin_q, cos_t=cos.T, sin_t=sin.T)


def _dft_tables(seq):
    split = math.gcd(seq, GRID_W)
    pos = jnp.arange(seq, dtype=jnp.int32)
    w = 2.0 * math.pi / seq

    def trig(k, axis):
        ang = ((k[:, None] * pos[None, :]) % seq).astype(F32) * w
        return jnp.expand_dims(jnp.cos(ang), axis), jnp.expand_dims(jnp.sin(ang), axis)

    ca, sa = trig(jnp.arange(seq // split, dtype=jnp.int32) * split, 1)
    cb, sb = trig(jnp.arange(split, dtype=jnp.int32), 0)
    cos = (ca * cb - sa * sb).reshape(seq, seq)
    neg_sin = (-(sa * cb + ca * sb)).reshape(seq, seq)
    return cos.astype(BF16), neg_sin.astype(BF16)


def _dft64():
    k = jnp.arange(F_GROUP_DIM, dtype=jnp.int32)
    ang = ((k[:, None] * k[None, :]) % F_GROUP_DIM).astype(F32) * (2.0 * math.pi / F_GROUP_DIM)
    eye = jnp.eye(F_GROUPS, dtype=F32)
    return jnp.concatenate([jnp.kron(eye, jnp.cos(ang)), jnp.kron(eye, jnp.sin(ang))], axis=1).astype(BF16)


def _gla_consts():
    idx = np.arange(GLA_TILE)
    same = (idx[:, None] // GLA_SUB) == (idx[None, :] // GLA_SUB)
    tril = same & (idx[None, :] <= idx[:, None])
    triu = same & (idx[None, :] >= idx[:, None])
    hq = np.arange(GLA_QK) // GLA_DK
    hv = np.arange(GLA_WIDTH) // GLA_DV
    e2 = hq[:, None] == hv[None, :]
    e3 = (hv[:, None] == hv[None, :]) / GLA_DV
    mask = hv[:, None] == hq[None, :]
    half = GLA_SUB // 2
    pair_q = np.concatenate([np.arange(GLA_SUB * half) // half, np.arange(half * half) // half])
    far_f = np.concatenate([np.zeros(GLA_SUB * half, int), np.full(half * half, half)])
    qi = np.arange(GLA_SUB)[:, None]
    sel_f = qi == (pair_q + far_f)[None, :]
    sel_b = qi == pair_q[None, :]
    return dict(tril=jnp.asarray(tril, BF16), triu=jnp.asarray(triu, BF16), tot=jnp.asarray(same, BF16),
                sel_f=jnp.asarray(sel_f, BF16), sel_b=jnp.asarray(sel_b, BF16),
                e2=jnp.asarray(e2, BF16), e3=jnp.asarray(e3, BF16), mask=jnp.asarray(mask, F32))


def _layer_weights(l, w_in, norm1_g, mla_q_norm_g, mla_kv_norm_g, w_q_up, w_kv_up, gla_wa2_f, gla_ba_f,
                   gla_wa2_b, gla_ba_b, gla_norm_g, w_o_fourier, w_o_mla, w_o_gla, w_out, norm2_g, w_ff1,
                   w_ff2, dft64):
    wi = w_in[l]
    o = np.cumsum([0, F_WIDTH, Q_LORA, KV_LORA, QK_ROPE, GLA_QK, GLA_QK, GLA_WIDTH, GLA_WIDTH,
                   GLA_GATE_RANK, GLA_GATE_RANK, N_BRANCH * D_MODEL])
    zf, zqd, zkvd, zkr, zgq, zgk, zgv, zgr, zaf, zab, zgate = (wi[:, int(o[i]):int(o[i + 1])] for i in range(11))
    small = jnp.concatenate([zkr, jnp.zeros((D_MODEL, S_AF - QK_ROPE), F32), zaf, zab,
                             jnp.zeros((D_MODEL, LANES - S_AB - GLA_GATE_RANK), F32)], axis=1)
    w1 = jnp.concatenate([zf, zqd, zkvd, zgq, zgk, small, zgv, zgr, zgate], axis=1).astype(BF16)
    wkrt = jnp.concatenate([zkr, _rot_partner(zkr)], axis=1).T.astype(BF16)

    wq3 = w_q_up[l].reshape(Q_LORA, MLA_HEADS, QK_NOPE + QK_ROPE)
    pad = jnp.zeros((Q_LORA, MLA_HEADS, HEAD_PAD - QK_NOPE - QK_ROPE), F32)
    wq = jnp.concatenate([wq3, pad], axis=2).reshape(Q_LORA, -1).astype(BF16)
    wqr = jnp.concatenate([jnp.zeros((Q_LORA, MLA_HEADS, QK_NOPE), F32), _rot_partner(wq3[..., QK_NOPE:]), pad],
                          axis=2).reshape(Q_LORA, -1).astype(BF16)
    wkv3 = w_kv_up[l].reshape(KV_LORA, MLA_HEADS, QK_NOPE + V_DIM)
    wkt = wkv3[..., :QK_NOPE].reshape(KV_LORA, -1).T.astype(BF16)
    wv = wkv3[..., QK_NOPE:].reshape(KV_LORA, -1).astype(BF16)

    wa = jnp.zeros((LANES, 2 * GLA_QK), F32)
    wa = wa.at[S_AF:S_AF + GLA_GATE_RANK, 0:GLA_QK].set(gla_wa2_f[l])
    wa = wa.at[S_AB:S_AB + GLA_GATE_RANK, GLA_QK:].set(gla_wa2_b[l]).astype(BF16)
    ba = jnp.concatenate([gla_ba_f[l], gla_ba_b[l]])[None]
    return dict(
        g1=norm1_g[l][None], w1=w1, dft64=dft64, gq=mla_q_norm_g[l][None], wq=wq, wqr=wqr,
        gkv=mla_kv_norm_g[l][None], wkt=wkt, wv=wv, wkrt=wkrt, wa=wa, ba=ba, gn=gla_norm_g[l][None],
        wof=w_o_fourier[l].astype(BF16), wom=w_o_mla[l].astype(BF16), wog=w_o_gla[l].astype(BF16),
        wout=w_out[l].astype(BF16), g2=norm2_g[l][None], wff1=w_ff1[l].astype(BF16),
        wff2=w_ff2[l].astype(BF16))


def _state_to_blockdiag_t(s):
    nb = s.shape[0]
    eye = jnp.eye(GLA_HEADS, dtype=s.dtype)
    st = jnp.einsum("bxhde,hg->bxhegd", s, eye)
    return st.reshape(nb, 2, GLA_WIDTH, GLA_QK)


def _blockdiag_t_to_state(st):
    nb = st.shape[0]
    s6 = st.reshape(nb, 2, GLA_HEADS, GLA_DV, GLA_HEADS, GLA_DK)
    diag = jnp.stack([s6[:, :, h, :, h, :] for h in range(GLA_HEADS)], axis=2)
    return jnp.swapaxes(diag, -1, -2)


def _pick_tile(seq, target):
    t = min(seq, target)
    assert seq % t == 0 and t % GLA_SUB == 0, (seq, t)
    return t


def _mixer_and_tail(x, mod, lw, tabs, dft, gla_c, cache, s0t, final_g, *, rope, final):
    nb, seq, _ = x.shape
    tm = _pick_tile(seq, 512)
    outs = _inproj_call(x, mod, lw, tabs, rope=rope, emit_ctx=not rope, tm=tm)
    fcs, q, kt, v, qk, gv, la, sg, gates = outs[:9]
    f = _fourier_call(fcs, dft[0], dft[1], tr=_pick_tile(seq, 512))
    tq = _pick_tile(seq, 1024)
    hp = MLA_HEADS if seq <= 512 else 2
    attn = _attn_call(q, kt, v, cache, tq=tq, ck=512, ru=min(tq, 128), hp=hp)
    og, st_out = _gla_call(qk, gv, la, sg, lw["gn"], s0t, gla_c)
    y = _tail_call(x, f, attn, og, gates, mod, lw, final_g, final=final, tm=tm)
    return y, outs[9:], st_out


def kernel(x_prompt, x_sample, c, cache_mla_ckv, cache_mla_krope, state_gla, c_ctx, w_mod, b_mod, norm1_g, w_in, mla_q_norm_g, mla_kv_norm_g, w_q_up, w_kv_up, gla_wa2_f, gla_ba_f, gla_wa2_b, gla_ba_b, gla_norm_g, w_o_fourier, w_o_mla, w_o_gla, w_out, norm2_g, w_ff1, w_ff2, final_norm_g):
    depth = w_mod.shape[0]
    n_ctx, seq_ctx, _ = x_prompt.shape
    n_lat, seq_lat, _ = x_sample.shape
    assert seq_ctx % GLA_TILE == 0 and seq_lat % GLA_TILE == 0 and seq_lat % GRID_W == 0
    assert 1 + n_lat <= MOD_ROWS

    cond = jnp.concatenate([c_ctx[None], c, jnp.zeros((MOD_ROWS - 1 - n_lat, D_MODEL), F32)], axis=0)
    mod_all = _mod_call(cond, w_mod, b_mod).reshape(depth, MOD_ROWS, N_MOD, D_MODEL)

    dft64 = _dft64()
    gla_c = _gla_consts()
    tabs = _rope_tables(seq_lat)
    dft_ctx = _dft_tables(seq_ctx)
    dft_lat = _dft_tables(seq_lat)
    final_g = final_norm_g[None]
    zero_state = jnp.zeros((n_ctx, 2, GLA_WIDTH, GLA_QK), F32)

    xp, xs = x_prompt, x_sample
    ckv_list, kr_list, gla_list = [], [], []
    for l in range(depth):
        lw = _layer_weights(l, w_in, norm1_g, mla_q_norm_g, mla_kv_norm_g, w_q_up, w_kv_up, gla_wa2_f,
                            gla_ba_f, gla_wa2_b, gla_ba_b, gla_norm_g, w_o_fourier, w_o_mla, w_o_gla,
                            w_out, norm2_g, w_ff1, w_ff2, dft64)
        final = l == depth - 1
        mod_ctx = mod_all[l, 0:1]
        mod_lat = mod_all[l, 1:1 + n_lat]

        xp, (ckv_c, kr_c), st_c = _mixer_and_tail(xp, mod_ctx, lw, None, dft_ctx, gla_c, None, zero_state,
                                                  final_g, rope=False, final=final)
        ckv_list.append(ckv_c)
        kr_list.append(kr_c)
        gla_list.append(_blockdiag_t_to_state(st_c))

        cache = _kvcache_call(cache_mla_ckv[:, l], jnp.swapaxes(cache_mla_krope[:, l], 1, 2), lw)
        s0t = _state_to_blockdiag_t(state_gla[:, l].astype(F32))
        xs, _, _ = _mixer_and_tail(xs, mod_lat, lw, tabs, dft_lat, gla_c, cache, s0t, final_g,
                                   rope=True, final=final)

    return (xp, xs, jnp.stack(ckv_list, axis=1), jnp.stack(kr_list, axis=1), jnp.stack(gla_list, axis=1))
```

```python
import functools
import math

import numpy as np
import jax
import jax.numpy as jnp
from jax import lax
from jax.experimental import pallas as pl
from jax.experimental.pallas import tpu as pltpu

F32 = jnp.float32
BF16 = jnp.bfloat16

D_MODEL = 1024
GRID_W = 64
F_GROUPS = 4
F_GROUP_DIM = 64
F_WIDTH = F_GROUPS * F_GROUP_DIM
MLA_HEADS = 8
QK_NOPE = 64
QK_ROPE = 32
V_DIM = 64
Q_LORA = 256
KV_LORA = 128
MLA_WIDTH = MLA_HEADS * V_DIM
GLA_HEADS = 4
GLA_DK = 32
GLA_DV = 64
GLA_WIDTH = GLA_HEADS * GLA_DV
GLA_QK = GLA_HEADS * GLA_DK
GLA_GATE_RANK = 16
GLA_TAU = 16.0
D_FF = 4 * D_MODEL
N_BRANCH = 3
ROPE_BASE = 10000.0
NORM_EPS = 1e-6
N_MOD = 6
LOG2_E = math.log2(math.e)

LANES = 128
VMEM_LIMIT_BYTES = 56 * 1024 * 1024

HEAD_PAD = LANES
ATTN_SLOTS = 3
GLA_SUB = 16
GLA_TILE = 128
GLA_DIAG = 8
GLA_PAIR_ROWS = GLA_SUB * GLA_DIAG
GLA_LEVELS = tuple(GLA_TILE >> s for s in range(1, (GLA_TILE // GLA_DIAG).bit_length()))
MOD_ROWS = 16
GLA_MASKED_LOG2 = -1e30

C_F = 0
C_QD = C_F + F_WIDTH
C_KVD = C_QD + Q_LORA
C_GQ = C_KVD + KV_LORA
C_GK = C_GQ + GLA_QK
C_SMALL = C_GK + GLA_QK
C_GV = C_SMALL + LANES
C_GR = C_GV + GLA_WIDTH
C_GATE = C_GR + GLA_WIDTH
W1_COLS = C_GATE + N_BRANCH * D_MODEL
S_KR = 0
S_AF = 64
S_AB = 80


def _dot(a, b):
    return jnp.dot(a, b, preferred_element_type=F32)


def _dot_nt(a, b):
    return lax.dot_general(a, b, (((1,), (1,)), ((), ())), preferred_element_type=F32)


def _dot_tn(a, b):
    return lax.dot_general(a, b, (((0,), (0,)), ((), ())), preferred_element_type=F32)


def _split3(x):
    hi = x.astype(BF16)
    r1 = x - hi.astype(F32)
    mid = r1.astype(BF16)
    lo = (r1 - mid.astype(F32)).astype(BF16)
    return hi, mid, lo


def _dot_exact_lhs(m, x):
    hi, mid, lo = _split3(x)
    return _dot(m, hi) + _dot(m, mid) + _dot(m, lo)


def _rms(x):
    return x * lax.rsqrt(jnp.mean(x * x, axis=-1, keepdims=True) + NORM_EPS)


def _sigmoid(x):
    return 1.0 / (1.0 + jnp.exp(-x))


def _const_spec(shape):
    return pl.BlockSpec(shape, lambda *_: (0,) * len(shape))


def _params(sem):
    return pltpu.CompilerParams(dimension_semantics=sem, vmem_limit_bytes=VMEM_LIMIT_BYTES)


def _mod_kernel(c_ref, w_ref, b_ref, o_ref):
    c = c_ref[...]
    s = (c * _sigmoid(c)).astype(BF16)
    o_ref[0] = _dot(s, w_ref[0].astype(BF16)) + b_ref[0]


def _mod_call(cond, w_mod, b_mod):
    depth = w_mod.shape[0]
    n_col = w_mod.shape[2] // D_MODEL
    return pl.pallas_call(
        _mod_kernel,
        grid=(depth, n_col),
        in_specs=[
            pl.BlockSpec((MOD_ROWS, D_MODEL), lambda l, j: (0, 0)),
            pl.BlockSpec((1, D_MODEL, D_MODEL), lambda l, j: (l, 0, j)),
            pl.BlockSpec((1, 1, D_MODEL), lambda l, j: (l, 0, j)),
        ],
        out_specs=pl.BlockSpec((1, MOD_ROWS, D_MODEL), lambda l, j: (l, 0, j)),
        out_shape=jax.ShapeDtypeStruct((depth, MOD_ROWS, w_mod.shape[2]), F32),
        compiler_params=_params(("arbitrary", "arbitrary")),
        name="mod_proj",
    )(cond, w_mod, b_mod.reshape(depth, 1, -1))


def _inproj_kernel(*refs, rope, emit_ctx):
    it = iter(refs)
    x_ref, mod_ref, g1_ref, w1_ref, dft_ref, gq_ref, wq_ref = (next(it) for _ in range(7))
    wqr_ref = next(it) if rope else None
    gkv_ref, wkt_ref, wv_ref, wkrt_ref, wa_ref, ba_ref = (next(it) for _ in range(6))
    if rope:
        cq_ref, sq_ref, ct_ref, st_ref = (next(it) for _ in range(4))
    fcs_ref, q_ref, kt_ref, v_ref, qk_ref, gv_ref, la_ref, sg_ref, gate_ref = (next(it) for _ in range(9))
    if emit_ctx:
        ckv_ref, kr_ref = next(it), next(it)

    x = x_ref[0]
    mod = mod_ref[0]
    shift1, scale1 = mod[0:1], mod[1:2]
    h = _rms(x) * g1_ref[...] * (1.0 + scale1) + shift1
    hb = h.astype(BF16)

    def branch_gate(br):
        lo = C_GATE + br * D_MODEL
        zg = _dot(hb, w1_ref[:, lo:lo + D_MODEL])
        gate_ref[0, :, br * D_MODEL:(br + 1) * D_MODEL] = _sigmoid(zg).astype(BF16)

    z_qd = _dot(hb, w1_ref[:, C_QD:C_QD + Q_LORA])
    z_kg = _dot(hb, w1_ref[:, C_KVD:C_KVD + KV_LORA + GLA_QK])
    z_ks = _dot(hb, w1_ref[:, C_GK:C_GK + GLA_QK + LANES])
    krt2 = _dot_nt(wkrt_ref[...], hb)
    branch_gate(0)

    cq = (_rms(z_qd) * gq_ref[...]).astype(BF16)
    qa = _dot(cq, wq_ref[...])
    scale = (QK_NOPE + QK_ROPE) ** -0.5 * LOG2_E
    if rope:
        qb = _dot(cq, wqr_ref[...])
        cos_q, sin_q = cq_ref[...], sq_ref[...]
        for hd in range(MLA_HEADS):
            sl = slice(hd * HEAD_PAD, (hd + 1) * HEAD_PAD)
            q_ref[0, :, sl] = ((qa[:, sl] * cos_q + qb[:, sl] * sin_q) * scale).astype(BF16)
    else:
        q_ref[0] = (qa * scale).astype(BF16)

    ckv = _rms(z_kg[:, 0:KV_LORA]) * gkv_ref[...]
    ckvb = ckv.astype(BF16)
    knt = _dot_nt(wkt_ref[...], ckvb)
    v_ref[0] = _dot(ckvb, wv_ref[...]).astype(BF16)
    krt = krt2[0:QK_ROPE]
    if rope:
        krt = krt * ct_ref[...] + krt2[QK_ROPE:2 * QK_ROPE] * st_ref[...]
    krt = krt.astype(BF16)
    zero_pad = jnp.zeros((HEAD_PAD - QK_NOPE - QK_ROPE, krt.shape[1]), BF16)
    for hd in range(MLA_HEADS):
        kt_ref[0, hd, 0:QK_NOPE, :] = knt[hd * QK_NOPE:(hd + 1) * QK_NOPE].astype(BF16)
        kt_ref[0, hd, QK_NOPE:QK_NOPE + QK_ROPE, :] = krt
        kt_ref[0, hd, QK_NOPE + QK_ROPE:HEAD_PAD, :] = zero_pad
    small = z_ks[:, GLA_QK:GLA_QK + LANES]
    if emit_ctx:
        ckv_ref[0] = ckv
        kr_ref[0] = small[:, S_KR:S_KR + QK_ROPE]
    branch_gate(1)

    zf = _dot(hb, w1_ref[:, C_F:C_F + F_WIDTH])

    qk_ref[0, :, 0:GLA_QK] = z_kg[:, KV_LORA:KV_LORA + GLA_QK] * (GLA_DK ** -0.5)
    qk_ref[0, :, GLA_QK:2 * GLA_QK] = z_ks[:, 0:GLA_QK]
    gv_ref[0] = _dot(hb, w1_ref[:, C_GV:C_GV + GLA_WIDTH])
    a_pre = _dot(small.astype(BF16), wa_ref[...]) + ba_ref[...]
    log_sig = jnp.minimum(a_pre, 0.0) - jnp.log(1.0 + jnp.exp(-jnp.abs(a_pre)))
    la_ref[0] = log_sig * (1.0 / GLA_TAU)
    zgr = _dot(hb, w1_ref[:, C_GR:C_GR + GLA_WIDTH])
    sg_ref[0] = zgr * _sigmoid(zgr)
    fcs_ref[0] = _dot(zf.astype(BF16), dft_ref[...]).astype(BF16)
    branch_gate(2)


def _inproj_call(x, mod, lw, tabs, *, rope, emit_ctx, tm):
    nb, seq, _ = x.shape
    nt = seq // tm
    mod_idx = (lambda b, t: (b, 0, 0)) if mod.shape[0] == nb else (lambda b, t: (0, 0, 0))
    tok = lambda w: pl.BlockSpec((1, tm, w), lambda b, t: (b, t, 0))

    ins = [x, mod, lw["g1"], lw["w1"], lw["dft64"], lw["gq"], lw["wq"]]
    specs = [tok(D_MODEL), pl.BlockSpec((1, N_MOD, D_MODEL), mod_idx),
             _const_spec((1, D_MODEL)), _const_spec((D_MODEL, W1_COLS)),
             _const_spec((F_WIDTH, 2 * F_WIDTH)), _const_spec((1, Q_LORA)),
             _const_spec((Q_LORA, MLA_HEADS * HEAD_PAD))]
    if rope:
        ins.append(lw["wqr"])
        specs.append(_const_spec((Q_LORA, MLA_HEADS * HEAD_PAD)))
    ins += [lw["gkv"], lw["wkt"], lw["wv"], lw["wkrt"], lw["wa"], lw["ba"]]
    specs += [_const_spec((1, KV_LORA)), _const_spec((MLA_HEADS * QK_NOPE, KV_LORA)),
              _const_spec((KV_LORA, MLA_WIDTH)), _const_spec((2 * QK_ROPE, D_MODEL)),
              _const_spec((LANES, 2 * GLA_QK)), _const_spec((1, 2 * GLA_QK))]
    if rope:
        ins += [tabs["cos_q"], tabs["sin_q"], tabs["cos_t"], tabs["sin_t"]]
        specs += [pl.BlockSpec((tm, HEAD_PAD), lambda b, t: (t, 0)),
                  pl.BlockSpec((tm, HEAD_PAD), lambda b, t: (t, 0)),
                  pl.BlockSpec((QK_ROPE, tm), lambda b, t: (0, t)),
                  pl.BlockSpec((QK_ROPE, tm), lambda b, t: (0, t))]

    def sds(w, dt):
        return jax.ShapeDtypeStruct((nb, seq, w), dt)

    out_shape = [sds(2 * F_WIDTH, BF16), sds(MLA_HEADS * HEAD_PAD, BF16),
                 jax.ShapeDtypeStruct((nb, MLA_HEADS, HEAD_PAD, seq), BF16),
                 sds(MLA_WIDTH, BF16), sds(2 * GLA_QK, F32), sds(GLA_WIDTH, F32),
                 sds(2 * GLA_QK, F32), sds(GLA_WIDTH, F32), sds(N_BRANCH * D_MODEL, BF16)]
    out_specs = [tok(2 * F_WIDTH), tok(MLA_HEADS * HEAD_PAD),
                 pl.BlockSpec((1, MLA_HEADS, HEAD_PAD, tm), lambda b, t: (b, 0, 0, t)),
                 tok(MLA_WIDTH), tok(2 * GLA_QK), tok(GLA_WIDTH), tok(2 * GLA_QK), tok(GLA_WIDTH),
                 tok(N_BRANCH * D_MODEL)]
    if emit_ctx:
        out_shape += [sds(KV_LORA, F32), sds(QK_ROPE, F32)]
        out_specs += [tok(KV_LORA), tok(QK_ROPE)]

    return pl.pallas_call(
        functools.partial(_inproj_kernel, rope=rope, emit_ctx=emit_ctx),
        grid=(nb, nt), in_specs=specs, out_specs=out_specs, out_shape=out_shape,
        compiler_params=_params(("parallel", "parallel")),
        name="in_proj_lat" if rope else "in_proj_ctx",
    )(*ins)


def _kvcache_kernel(ckv_ref, krt_ref, wkt_ref, wv_ref, kt_ref, v_ref):
    ckvb = ckv_ref[0].astype(BF16)
    knt = _dot_nt(wkt_ref[...], ckvb)
    v_ref[0] = _dot(ckvb, wv_ref[...]).astype(BF16)
    krt = krt_ref[0].astype(BF16)
    zero_pad = jnp.zeros((HEAD_PAD - QK_NOPE - QK_ROPE, krt.shape[1]), BF16)
    for hd in range(MLA_HEADS):
        kt_ref[0, hd, 0:QK_NOPE, :] = knt[hd * QK_NOPE:(hd + 1) * QK_NOPE].astype(BF16)
        kt_ref[0, hd, QK_NOPE:QK_NOPE + QK_ROPE, :] = krt
        kt_ref[0, hd, QK_NOPE + QK_ROPE:HEAD_PAD, :] = zero_pad


def _kvcache_call(ckv, krt, lw):
    nb, past, _ = ckv.shape
    return pl.pallas_call(
        _kvcache_kernel,
        grid=(nb,),
        in_specs=[pl.BlockSpec((1, past, KV_LORA), lambda b: (b, 0, 0)),
                  pl.BlockSpec((1, QK_ROPE, past), lambda b: (b, 0, 0)),
                  _const_spec((MLA_HEADS * QK_NOPE, KV_LORA)), _const_spec((KV_LORA, MLA_WIDTH))],
        out_specs=[pl.BlockSpec((1, MLA_HEADS, HEAD_PAD, past), lambda b: (b, 0, 0, 0)),
                   pl.BlockSpec((1, past, MLA_WIDTH), lambda b: (b, 0, 0))],
        out_shape=[jax.ShapeDtypeStruct((nb, MLA_HEADS, HEAD_PAD, past), BF16),
                   jax.ShapeDtypeStruct((nb, past, MLA_WIDTH), BF16)],
        compiler_params=_params(("parallel",)),
        name="kv_cache",
    )(ckv, krt, lw["wkt"], lw["wv"])


def _fourier_kernel(c_ref, ns_ref, fcs_ref, o_ref, *, norm):
    fc = fcs_ref[0, :, 0:F_WIDTH]
    fs = fcs_ref[0, :, F_WIDTH:2 * F_WIDTH]
    o_ref[0] = ((_dot(c_ref[...], fc) + _dot(ns_ref[...], fs)) * norm).astype(BF16)


def _fourier_call(fcs, dft_c, dft_ns, *, tr):
    nb, seq, _ = fcs.shape
    norm = 1.0 / math.sqrt(seq * F_GROUP_DIM)
    return pl.pallas_call(
        functools.partial(_fourier_kernel, norm=norm),
        grid=(seq // tr, nb),
        in_specs=[pl.BlockSpec((tr, seq), lambda i, b: (i, 0)),
                  pl.BlockSpec((tr, seq), lambda i, b: (i, 0)),
                  pl.BlockSpec((1, seq, 2 * F_WIDTH), lambda i, b: (b, 0, 0))],
        out_specs=pl.BlockSpec((1, tr, F_WIDTH), lambda i, b: (b, i, 0)),
        out_shape=jax.ShapeDtypeStruct((nb, seq, F_WIDTH), BF16),
        compiler_params=_params(("parallel", "parallel")),
        name="fourier",
    )(dft_c, dft_ns, fcs)


def _attn_kernel(*refs, has_cache, ck, ru, hp):
    if has_cache:
        q_ref, kt_ref, v_ref, ktc_ref, vc_ref, o_ref, s_scr, p_scr = refs
    else:
        q_ref, kt_ref, v_ref, o_ref, s_scr, p_scr = refs
        ktc_ref = vc_ref = None

    chunks = []
    col = 0
    for k_r, v_r in ((ktc_ref, vc_ref), (kt_ref, v_ref)):
        if k_r is None:
            continue
        for off in range(0, k_r.shape[3], ck):
            w = min(ck, k_r.shape[3] - off)
            chunks.append((k_r, v_r, off, col, w))
            col += w

    def fold(t, x, op):
        for c in range(0, x.shape[1], LANES):
            piece = x[:, c:c + LANES]
            t = piece if t is None else op(t, piece)
        return t

    def scores(t, chunk, m_t):
        j, rows = units[t]
        k_r, _, off, c0, w = chunk
        s = _dot(q_ref[0, rows, j * HEAD_PAD:(j + 1) * HEAD_PAD], k_r[0, j, :, off:off + w])
        s_scr[t % ATTN_SLOTS, :, c0:c0 + w] = s
        return fold(m_t, s, jnp.maximum)

    def probs(t, chunk, m, l_t):
        _, _, _, c0, w = chunk
        p = jnp.exp2(s_scr[t % ATTN_SLOTS, :, c0:c0 + w] - m)
        p_scr[t % ATTN_SLOTS, :, c0:c0 + w] = p.astype(BF16)
        return fold(l_t, p, jnp.add)

    def weighted(t, chunk, acc):
        _, v_r, off, c0, w = chunk
        pair = units[t][0] // 2
        pv = _dot(p_scr[t % ATTN_SLOTS, :, c0:c0 + w], v_r[0, off:off + w, pair * LANES:(pair + 1) * LANES])
        return pv if acc is None else acc + pv

    tq = q_ref.shape[1]
    units = [(j, slice(r0, r0 + ru)) for r0 in range(0, tq, ru) for j in range(hp)]
    n_u = len(units)
    m_t, l_t, acc, m = ([None] * n_u for _ in range(4))
    lane = lax.broadcasted_iota(jnp.int32, (ru, LANES), 1)
    for t in range(n_u + 2):
        if 1 <= t <= n_u:
            m[t - 1] = jnp.max(m_t[t - 1], axis=-1, keepdims=True)
        for chunk in chunks:
            if t < n_u:
                m_t[t] = scores(t, chunk, m_t[t])
            if 1 <= t <= n_u:
                l_t[t - 1] = probs(t - 1, chunk, m[t - 1], l_t[t - 1])
            if 2 <= t:
                acc[t - 2] = weighted(t - 2, chunk, acc[t - 2])
        u = t - 2
        if u >= 0 and units[u][0] % 2 == 1:
            j, rows = units[u]
            even, odd = (acc[v] / jnp.sum(l_t[v], axis=-1, keepdims=True) for v in (u - 1, u))
            o_ref[0, rows, (j // 2) * LANES:(j // 2 + 1) * LANES] = jnp.where(lane < V_DIM, even, odd).astype(BF16)


def _attn_call(q, kt, v, cache, *, tq, ck, ru, hp):
    nb, seq, _ = q.shape
    has_cache = cache is not None
    n_keys = seq + (cache[0].shape[3] if has_cache else 0)
    assert hp % 2 == 0 and MLA_HEADS % hp == 0 and 2 * V_DIM == LANES
    val_w = hp * V_DIM
    ins = [q, kt, v]
    specs = [pl.BlockSpec((1, tq, hp * HEAD_PAD), lambda b, p, i: (b, i, p)),
             pl.BlockSpec((1, hp, HEAD_PAD, seq), lambda b, p, i: (b, p, 0, 0)),
             pl.BlockSpec((1, seq, val_w), lambda b, p, i: (b, 0, p))]
    if has_cache:
        past = cache[0].shape[3]
        ins += list(cache)
        specs += [pl.BlockSpec((1, hp, HEAD_PAD, past), lambda b, p, i: (b, p, 0, 0)),
                  pl.BlockSpec((1, past, val_w), lambda b, p, i: (b, 0, p))]
    return pl.pallas_call(
        functools.partial(_attn_kernel, has_cache=has_cache, ck=ck, ru=ru, hp=hp),
        grid=(nb, MLA_HEADS // hp, seq // tq),
        in_specs=specs,
        out_specs=pl.BlockSpec((1, tq, val_w), lambda b, p, i: (b, i, p)),
        out_shape=jax.ShapeDtypeStruct((nb, seq, MLA_WIDTH), BF16),
        scratch_shapes=[pltpu.VMEM((ATTN_SLOTS, ru, n_keys), F32),
                        pltpu.VMEM((ATTN_SLOTS, ru, n_keys), BF16)],
        compiler_params=_params(("parallel", "parallel", "parallel")),
        name="attention_lat" if has_cache else "attention_ctx",
    )(*ins)


def _gla_tile(la, q, k, v, st_ref, cum_ref, sel_ref, e2_ref, mask_ref, kmask_ref, vmask_ref, lmask_refs, *,
              reverse):
    cum = _dot_exact_lhs(cum_ref[...], la) * LOG2_E
    yield None

    tot = cum[0:1] if reverse else cum[GLA_TILE - 1:GLA_TILE]
    qe = (q * jnp.exp2(cum)).astype(BF16)
    ke = (k * jnp.exp2(tot - cum)).astype(BF16)
    jrow = lax.broadcasted_iota(jnp.int32, (GLA_DIAG, GLA_QK), 0)
    n_sub = GLA_TILE // GLA_SUB
    subs = [slice(n * GLA_SUB, (n + 1) * GLA_SUB) for n in range(n_sub)]

    st = st_ref[...]
    inter = _dot_nt(qe, st.astype(BF16))
    st_ref[...] = st * jnp.exp2(tot) + _dot_tn(v.astype(BF16), ke) * mask_ref[...]

    rect = []
    for hl in GLA_LEVELS:
        qs, ks, vs, q_rows = [], [], [], []
        for lo in range(0, GLA_TILE, 2 * hl):
            mid, hi = lo + hl, lo + 2 * hl
            if reverse:
                edge, qr, kr = cum[mid:mid + 1], slice(lo, mid), slice(mid, hi)
            else:
                edge, qr, kr = cum[mid - 1:mid], slice(mid, hi), slice(lo, mid)
            qs.append(q[qr] * jnp.exp2(cum[qr] - edge))
            ks.append(k[kr] * jnp.exp2(edge - cum[kr]))
            vs.append(v[kr])
            q_rows.append(qr)
        k_half = jnp.concatenate(ks, axis=0)
        k_bd = (jnp.concatenate([k_half] * GLA_HEADS, axis=0) * kmask_ref[...]).astype(BF16)
        sc = _dot_nt(jnp.concatenate(qs, axis=0).astype(BF16), k_bd)
        rect.append((sc, jnp.concatenate(vs, axis=0), q_rows))
    yield None

    dg = GLA_DIAG
    spread = []
    for r in subs:
        pieces = []
        for b0 in range(r.start, r.stop, dg):
            bq, kk = cum[b0:b0 + dg], k[b0:b0 + dg]
            for i in range(dg):
                keep = (jrow >= i) if reverse else (jrow <= i)
                d = jnp.where(keep, bq[i:i + 1] - bq, GLA_MASKED_LOG2)
                pieces.append(q[b0 + i:b0 + i + 1] * kk * jnp.exp2(d))
        wmat = jnp.concatenate(pieces, axis=0).astype(BF16)
        spread.append(_dot(wmat, e2_ref[...]))
    yield None

    intra = []
    for r, a in zip(subs, spread):
        av = jnp.concatenate([
            (a[n * dg * dg:(n + 1) * dg * dg].reshape(dg, dg, GLA_WIDTH)
             * v[r.start + n * dg:r.start + (n + 1) * dg][None]).reshape(dg * dg, GLA_WIDTH)
            for n in range(GLA_SUB // dg)], axis=0)
        intra.append(_dot(sel_ref[...], av.astype(BF16)))
    crosses = []
    for (sc, v_half, q_rows), lmask_ref in zip(rect, lmask_refs):
        p = sc if lmask_ref is None else sc * lmask_ref[...]
        v_bd = (jnp.concatenate([v_half] * GLA_HEADS, axis=0) * vmask_ref[...]).astype(BF16)
        crosses.append((_dot(p.astype(BF16), v_bd), q_rows))
    yield None

    diag = jnp.concatenate(intra, axis=0)
    outs = [inter[r0:r0 + dg] + diag[r0:r0 + dg] for r0 in range(0, GLA_TILE, dg)]
    for cross, q_rows in crosses:
        at = 0
        for qr in q_rows:
            for r0 in range(qr.start, qr.stop, dg):
                outs[r0 // dg] = outs[r0 // dg] + cross[at:at + dg]
                at += dg
    yield jnp.concatenate(outs, axis=0)


def _alternate(*stage_generators):
    last = [None] * len(stage_generators)
    live = list(range(len(stage_generators)))
    while live:
        for g in list(live):
            try:
                last[g] = next(stage_generators[g])
            except StopIteration:
                live.remove(g)
    return last


def _gla_kernel(qk_ref, gv_ref, la_ref, sg_ref, gn_ref, s0_ref, tril_ref, triu_ref, sel_ref,
                e2_ref, e3_ref, mask_ref, kmask_ref, vmask_ref, *rest):
    lmask_refs = (None,) + tuple(rest[:len(GLA_LEVELS) - 1])
    og_ref, sout_ref, of_scr, ob_scr, stf_scr, stb_scr = rest[len(GLA_LEVELS) - 1:]
    seq = qk_ref.shape[1]
    n_tiles = seq // GLA_TILE

    def tile_rows(t):
        return pl.ds(pl.multiple_of(t * GLA_TILE, GLA_TILE), GLA_TILE)

    def load(rows, lo):
        return (la_ref[0, rows, lo:lo + GLA_QK], qk_ref[0, rows, 0:GLA_QK],
                qk_ref[0, rows, GLA_QK:2 * GLA_QK], gv_ref[0, rows, :])

    stf_scr[...] = s0_ref[0, 0]
    stb_scr[...] = s0_ref[0, 1]

    def sweep(t, carry):
        rf = tile_rows(t)
        rb = tile_rows(n_tiles - 1 - t)
        shared = (sel_ref, e2_ref, mask_ref, kmask_ref, vmask_ref, lmask_refs)
        of_scr[rf, :], ob_scr[rb, :] = _alternate(
            _gla_tile(*load(rf, 0), stf_scr, tril_ref, *shared, reverse=False),
            _gla_tile(*load(rb, GLA_QK), stb_scr, triu_ref, *shared, reverse=True))
        return carry

    lax.fori_loop(0, n_tiles, sweep, 0)
    sout_ref[0, 0] = stf_scr[...]
    sout_ref[0, 1] = stb_scr[...]

    def finish(t, carry):
        rows = tile_rows(t)
        o = of_scr[rows, :] + ob_scr[rows, :]
        o2 = o * o
        hi = o2.astype(BF16)
        lo = (o2 - hi.astype(F32)).astype(BF16)
        ms = _dot(hi, e3_ref[...]) + _dot(lo, e3_ref[...])
        og = o * lax.rsqrt(ms + NORM_EPS) * gn_ref[...] * sg_ref[0, rows, :]
        og_ref[0, rows, :] = og.astype(BF16)
        return carry

    lax.fori_loop(0, n_tiles, finish, 0)


def _gla_call(qk, gv, la, sg, gnorm, s0t, consts):
    nb, seq, _ = qk.shape
    tok = lambda w: pl.BlockSpec((1, seq, w), lambda b: (b, 0, 0))
    st_spec = pl.BlockSpec((1, 2, GLA_WIDTH, GLA_QK), lambda b: (b, 0, 0, 0))
    half_tile = GLA_TILE // 2
    return pl.pallas_call(
        _gla_kernel,
        grid=(nb,),
        in_specs=[tok(2 * GLA_QK), tok(GLA_WIDTH), tok(2 * GLA_QK), tok(GLA_WIDTH),
                  _const_spec((1, GLA_WIDTH)), st_spec,
                  _const_spec((GLA_TILE, GLA_TILE)), _const_spec((GLA_TILE, GLA_TILE)),
                  _const_spec((GLA_SUB, GLA_PAIR_ROWS)), _const_spec((GLA_QK, GLA_WIDTH)),
                  _const_spec((GLA_WIDTH, GLA_WIDTH)), _const_spec((GLA_WIDTH, GLA_QK)),
                  _const_spec((GLA_HEADS * half_tile, GLA_QK)), _const_spec((GLA_HEADS * half_tile, GLA_WIDTH))]
                 + [_const_spec((half_tile, GLA_HEADS * half_tile))] * (len(GLA_LEVELS) - 1),
        out_specs=[tok(GLA_WIDTH), st_spec],
        out_shape=[jax.ShapeDtypeStruct((nb, seq, GLA_WIDTH), BF16),
                   jax.ShapeDtypeStruct((nb, 2, GLA_WIDTH, GLA_QK), F32)],
        scratch_shapes=[pltpu.VMEM((seq, GLA_WIDTH), F32), pltpu.VMEM((seq, GLA_WIDTH), F32),
                        pltpu.VMEM((GLA_WIDTH, GLA_QK), F32), pltpu.VMEM((GLA_WIDTH, GLA_QK), F32)],
        compiler_params=_params(("parallel",)),
        name="gla",
    )(qk, gv, la, sg, gnorm, s0t, consts["tril"], consts["triu"], consts["sel"],
      consts["e2"], consts["e3"], consts["mask"], consts["kmask"], consts["vmask"], *consts["lmasks"])


def _tail_kernel(x_ref, f_ref, at_ref, og_ref, gate_ref, mod_ref, wof_ref, wom_ref, wog_ref, wout_ref,
                 g2_ref, w1_ref, w2_ref, gf_ref, o_ref, *, final, ff_chunk):
    mod = mod_ref[0]
    gate1, shift2, scale2, gate2 = mod[2:3], mod[3:4], mod[4:5], mod[5:6]
    ya = _dot(f_ref[0], wof_ref[...])
    yb = _dot(at_ref[0], wom_ref[...])
    yc = _dot(og_ref[0], wog_ref[...])
    merged = (gate_ref[0, :, 0:D_MODEL].astype(F32) * ya
              + gate_ref[0, :, D_MODEL:2 * D_MODEL].astype(F32) * yb
              + gate_ref[0, :, 2 * D_MODEL:3 * D_MODEL].astype(F32) * yc)
    x1 = x_ref[0] + gate1 * _dot(merged.astype(BF16), wout_ref[...])
    h2 = (_rms(x1) * g2_ref[...] * (1.0 + scale2) + shift2).astype(BF16)
    acc = None
    for c0 in range(0, D_FF, ff_chunk):
        u = jnp.maximum(_dot(h2, w1_ref[:, c0:c0 + ff_chunk]), 0.0)
        part = _dot((u * u).astype(BF16), w2_ref[c0:c0 + ff_chunk, :])
        acc = part if acc is None else acc + part
    x2 = x1 + gate2 * acc
    if final:
        x2 = _rms(x2) * gf_ref[...]
    o_ref[0] = x2


def _tail_call(x, f, attn, og, gates, mod, lw, final_g, *, final, tm):
    nb, seq, _ = x.shape
    mod_idx = (lambda b, t: (b, 0, 0)) if mod.shape[0] == nb else (lambda b, t: (0, 0, 0))
    tok = lambda w: pl.BlockSpec((1, tm, w), lambda b, t: (b, t, 0))
    return pl.pallas_call(
        functools.partial(_tail_kernel, final=final, ff_chunk=1024),
        grid=(nb, seq // tm),
        in_specs=[tok(D_MODEL), tok(F_WIDTH), tok(MLA_WIDTH), tok(GLA_WIDTH), tok(N_BRANCH * D_MODEL),
                  pl.BlockSpec((1, N_MOD, D_MODEL), mod_idx),
                  _const_spec((F_WIDTH, D_MODEL)), _const_spec((MLA_WIDTH, D_MODEL)),
                  _const_spec((GLA_WIDTH, D_MODEL)), _const_spec((D_MODEL, D_MODEL)),
                  _const_spec((1, D_MODEL)), _const_spec((D_MODEL, D_FF)), _const_spec((D_FF, D_MODEL)),
                  _const_spec((1, D_MODEL))],
        out_specs=tok(D_MODEL),
        out_shape=jax.ShapeDtypeStruct((nb, seq, D_MODEL), F32),
        compiler_params=_params(("parallel", "parallel")),
        name="tail",
    )(x, f, attn, og, gates, mod, lw["wof"], lw["wom"], lw["wog"], lw["wout"], lw["g2"], lw["wff1"],
      lw["wff2"], final_g)


def _rot_partner(w):
    q = QK_ROPE // 4
    return jnp.concatenate([-w[..., q:2 * q], w[..., 0:q], -w[..., 3 * q:4 * q], w[..., 2 * q:3 * q]], axis=-1)


def _rope_tables(seq):
    half = QK_ROPE // 2
    pos = jnp.arange(seq)
    row = (pos // GRID_W).astype(F32)
    col = (pos % GRID_W).astype(F32)
    inv = ROPE_BASE ** (-jnp.arange(0, half, 2, dtype=F32) / half)
    ang = jnp.concatenate([row[:, None] * inv, row[:, None] * inv, col[:, None] * inv, col[:, None] * inv], axis=1)
    cos, sin = jnp.cos(ang), jnp.sin(ang)
    pad = HEAD_PAD - QK_NOPE - QK_ROPE
    cos_q = jnp.concatenate([jnp.ones((seq, QK_NOPE), F32), cos, jnp.zeros((seq, pad), F32)], axis=1)
    sin_q = jnp.concatenate([jnp.zeros((seq, QK_NOPE), F32), sin, jnp.zeros((seq, pad), F32)], axis=1)
    return dict(cos_q=cos_q, sin_q=sin_q, cos_t=cos.T, sin_t=sin.T)


def _dft_tables(seq):
    split = math.gcd(seq, GRID_W)
    pos = jnp.arange(seq, dtype=jnp.int32)
    w = 2.0 * math.pi / seq

    def trig(k, axis):
        ang = ((k[:, None] * pos[None, :]) % seq).astype(F32) * w
        return jnp.expand_dims(jnp.cos(ang), axis), jnp.expand_dims(jnp.sin(ang), axis)

    ca, sa = trig(jnp.arange(seq // split, dtype=jnp.int32) * split, 1)
    cb, sb = trig(jnp.arange(split, dtype=jnp.int32), 0)
    cos = (ca * cb - sa * sb).reshape(seq, seq)
    neg_sin = (-(sa * cb + ca * sb)).reshape(seq, seq)
    return cos.astype(BF16), neg_sin.astype(BF16)


def _dft64():
    k = jnp.arange(F_GROUP_DIM, dtype=jnp.int32)
    ang = ((k[:, None] * k[None, :]) % F_GROUP_DIM).astype(F32) * (2.0 * math.pi / F_GROUP_DIM)
    eye = jnp.eye(F_GROUPS, dtype=F32)
    return jnp.concatenate([jnp.kron(eye, jnp.cos(ang)), jnp.kron(eye, jnp.sin(ang))], axis=1).astype(BF16)


def _gla_consts():
    idx = np.arange(GLA_TILE)
    tril = idx[None, :] <= idx[:, None]
    triu = idx[None, :] >= idx[:, None]
    hq = np.arange(GLA_QK) // GLA_DK
    hv = np.arange(GLA_WIDTH) // GLA_DV
    e2 = hq[:, None] == hv[None, :]
    e3 = (hv[:, None] == hv[None, :]) / GLA_DV
    mask = hv[:, None] == hq[None, :]
    sel = np.arange(GLA_SUB)[:, None] == (np.arange(GLA_PAIR_ROWS) // GLA_DIAG)[None, :]
    half_tile = GLA_TILE // 2
    row_head = np.arange(GLA_HEADS * half_tile) // half_tile
    kmask = row_head[:, None] == hq[None, :]
    vmask = row_head[:, None] == hv[None, :]
    pos = np.arange(half_tile)
    key_pos = np.tile(pos, GLA_HEADS)
    lmasks = [jnp.asarray((pos[:, None] // hl) == (key_pos[None, :] // hl), F32) for hl in GLA_LEVELS[1:]]
    return dict(tril=jnp.asarray(tril, BF16), triu=jnp.asarray(triu, BF16),
                sel=jnp.asarray(sel, BF16),
                e2=jnp.asarray(e2, BF16), e3=jnp.asarray(e3, BF16), mask=jnp.asarray(mask, F32),
                kmask=jnp.asarray(kmask, F32), vmask=jnp.asarray(vmask, F32), lmasks=lmasks)


def _layer_weights(l, w_in, norm1_g, mla_q_norm_g, mla_kv_norm_g, w_q_up, w_kv_up, gla_wa2_f, gla_ba_f,
                   gla_wa2_b, gla_ba_b, gla_norm_g, w_o_fourier, w_o_mla, w_o_gla, w_out, norm2_g, w_ff1,
                   w_ff2, dft64):
    wi = w_in[l]
    o = np.cumsum([0, F_WIDTH, Q_LORA, KV_LORA, QK_ROPE, GLA_QK, GLA_QK, GLA_WIDTH, GLA_WIDTH,
                   GLA_GATE_RANK, GLA_GATE_RANK, N_BRANCH * D_MODEL])
    zf, zqd, zkvd, zkr, zgq, zgk, zgv, zgr, zaf, zab, zgate = (wi[:, int(o[i]):int(o[i + 1])] for i in range(11))
    small = jnp.concatenate([zkr, jnp.zeros((D_MODEL, S_AF - QK_ROPE), F32), zaf, zab,
                             jnp.zeros((D_MODEL, LANES - S_AB - GLA_GATE_RANK), F32)], axis=1)
    w1 = jnp.concatenate([zf, zqd, zkvd, zgq, zgk, small, zgv, zgr, zgate], axis=1).astype(BF16)
    wkrt = jnp.concatenate([zkr, _rot_partner(zkr)], axis=1).T.astype(BF16)

    wq3 = w_q_up[l].reshape(Q_LORA, MLA_HEADS, QK_NOPE + QK_ROPE)
    pad = jnp.zeros((Q_LORA, MLA_HEADS, HEAD_PAD - QK_NOPE - QK_ROPE), F32)
    wq = jnp.concatenate([wq3, pad], axis=2).reshape(Q_LORA, -1).astype(BF16)
    wqr = jnp.concatenate([jnp.zeros((Q_LORA, MLA_HEADS, QK_NOPE), F32), _rot_partner(wq3[..., QK_NOPE:]), pad],
                          axis=2).reshape(Q_LORA, -1).astype(BF16)
    wkv3 = w_kv_up[l].reshape(KV_LORA, MLA_HEADS, QK_NOPE + V_DIM)
    wkt = wkv3[..., :QK_NOPE].reshape(KV_LORA, -1).T.astype(BF16)
    wv = wkv3[..., QK_NOPE:].reshape(KV_LORA, -1).astype(BF16)

    wa = jnp.zeros((LANES, 2 * GLA_QK), F32)
    wa = wa.at[S_AF:S_AF + GLA_GATE_RANK, 0:GLA_QK].set(gla_wa2_f[l])
    wa = wa.at[S_AB:S_AB + GLA_GATE_RANK, GLA_QK:].set(gla_wa2_b[l]).astype(BF16)
    ba = jnp.concatenate([gla_ba_f[l], gla_ba_b[l]])[None]
    return dict(
        g1=norm1_g[l][None], w1=w1, dft64=dft64, gq=mla_q_norm_g[l][None], wq=wq, wqr=wqr,
        gkv=mla_kv_norm_g[l][None], wkt=wkt, wv=wv, wkrt=wkrt, wa=wa, ba=ba, gn=gla_norm_g[l][None],
        wof=w_o_fourier[l].astype(BF16), wom=w_o_mla[l].astype(BF16), wog=w_o_gla[l].astype(BF16),
        wout=w_out[l].astype(BF16), g2=norm2_g[l][None], wff1=w_ff1[l].astype(BF16),
        wff2=w_ff2[l].astype(BF16))


def _state_to_blockdiag_t(s):
    nb = s.shape[0]
    eye = jnp.eye(GLA_HEADS, dtype=s.dtype)
    st = jnp.einsum("bxhde,hg->bxhegd", s, eye)
    return st.reshape(nb, 2, GLA_WIDTH, GLA_QK)


def _blockdiag_t_to_state(st):
    nb = st.shape[0]
    s6 = st.reshape(nb, 2, GLA_HEADS, GLA_DV, GLA_HEADS, GLA_DK)
    diag = jnp.stack([s6[:, :, h, :, h, :] for h in range(GLA_HEADS)], axis=2)
    return jnp.swapaxes(diag, -1, -2)


def _pick_tile(seq, target):
    t = min(seq, target)
    assert seq % t == 0 and t % GLA_SUB == 0, (seq, t)
    return t


def _mixer_and_tail(x, mod, lw, tabs, dft, gla_c, cache, s0t, final_g, *, rope, final):
    nb, seq, _ = x.shape
    tm = _pick_tile(seq, 512)
    outs = _inproj_call(x, mod, lw, tabs, rope=rope, emit_ctx=not rope, tm=tm)
    fcs, q, kt, v, qk, gv, la, sg, gates = outs[:9]
    f = _fourier_call(fcs, dft[0], dft[1], tr=_pick_tile(seq, 512))
    tq = _pick_tile(seq, 1024)
    hp = MLA_HEADS if seq <= 512 else 2
    attn = _attn_call(q, kt, v, cache, tq=tq, ck=512, ru=min(tq, 128), hp=hp)
    og, st_out = _gla_call(qk, gv, la, sg, lw["gn"], s0t, gla_c)
    y = _tail_call(x, f, attn, og, gates, mod, lw, final_g, final=final, tm=tm)
    return y, outs[9:], st_out


def kernel(x_prompt, x_sample, c, cache_mla_ckv, cache_mla_krope, state_gla, c_ctx, w_mod, b_mod, norm1_g, w_in, mla_q_norm_g, mla_kv_norm_g, w_q_up, w_kv_up, gla_wa2_f, gla_ba_f, gla_wa2_b, gla_ba_b, gla_norm_g, w_o_fourier, w_o_mla, w_o_gla, w_out, norm2_g, w_ff1, w_ff2, final_norm_g):
    depth = w_mod.shape[0]
    n_ctx, seq_ctx, _ = x_prompt.shape
    n_lat, seq_lat, _ = x_sample.shape
    assert seq_ctx % GLA_TILE == 0 and seq_lat % GLA_TILE == 0 and seq_lat % GRID_W == 0
    assert 1 + n_lat <= MOD_ROWS

    cond = jnp.concatenate([c_ctx[None], c, jnp.zeros((MOD_ROWS - 1 - n_lat, D_MODEL), F32)], axis=0)
    mod_all = _mod_call(cond, w_mod, b_mod).reshape(depth, MOD_ROWS, N_MOD, D_MODEL)

    dft64 = _dft64()
    gla_c = _gla_consts()
    tabs = _rope_tables(seq_lat)
    dft_ctx = _dft_tables(seq_ctx)
    dft_lat = _dft_tables(seq_lat)
    final_g = final_norm_g[None]
    zero_state = jnp.zeros((n_ctx, 2, GLA_WIDTH, GLA_QK), F32)

    xp, xs = x_prompt, x_sample
    ckv_list, kr_list, gla_list = [], [], []
    for l in range(depth):
        lw = _layer_weights(l, w_in, norm1_g, mla_q_norm_g, mla_kv_norm_g, w_q_up, w_kv_up, gla_wa2_f,
                            gla_ba_f, gla_wa2_b, gla_ba_b, gla_norm_g, w_o_fourier, w_o_mla, w_o_gla,
                            w_out, norm2_g, w_ff1, w_ff2, dft64)
        final = l == depth - 1
        mod_ctx = mod_all[l, 0:1]
        mod_lat = mod_all[l, 1:1 + n_lat]

        xp, (ckv_c, kr_c), st_c = _mixer_and_tail(xp, mod_ctx, lw, None, dft_ctx, gla_c, None, zero_state,
                                                  final_g, rope=False, final=final)
        ckv_list.append(ckv_c)
        kr_list.append(kr_c)
        gla_list.append(_blockdiag_t_to_state(st_c))

        cache = _kvcache_call(cache_mla_ckv[:, l], jnp.swapaxes(cache_mla_krope[:, l], 1, 2), lw)
        s0t = _state_to_blockdiag_t(state_gla[:, l].astype(F32))
        xs, _, _ = _mixer_and_tail(xs, mod_lat, lw, tabs, dft_lat, gla_c, cache, s0t, final_g,
                                   rope=True, final=final)

    return (xp, xs, jnp.stack(ckv_list, axis=1), jnp.stack(kr_list, axis=1), jnp.stack(gla_list, axis=1))
```

```python
import functools
import math

import numpy as np
import jax
import jax.numpy as jnp
from jax import lax
from jax.experimental import pallas as pl
from jax.experimental.pallas import tpu as pltpu

F32 = jnp.float32
BF16 = jnp.bfloat16

D_MODEL = 1024
GRID_W = 64
F_GROUPS = 4
F_GROUP_DIM = 64
F_WIDTH = F_GROUPS * F_GROUP_DIM
MLA_HEADS = 8
QK_NOPE = 64
QK_ROPE = 32
V_DIM = 64
Q_LORA = 256
KV_LORA = 128
MLA_WIDTH = MLA_HEADS * V_DIM
GLA_HEADS = 4
GLA_DK = 32
GLA_DV = 64
GLA_WIDTH = GLA_HEADS * GLA_DV
GLA_QK = GLA_HEADS * GLA_DK
GLA_GATE_RANK = 16
GLA_TAU = 16.0
D_FF = 4 * D_MODEL
N_BRANCH = 3
ROPE_BASE = 10000.0
NORM_EPS = 1e-6
N_MOD = 6
LOG2_E = math.log2(math.e)

LANES = 128
VMEM_LIMIT_BYTES = 56 * 1024 * 1024

HEAD_PAD = LANES
ATTN_SLOTS = 3
GLA_SUB = 16
GLA_TILE = 128
GLA_DIAG = 8
GLA_PAIR_ROWS = GLA_SUB * GLA_DIAG
GLA_LEVELS = tuple(GLA_TILE >> s for s in range(1, (GLA_TILE // GLA_DIAG).bit_length()))
MOD_ROWS = 16
GLA_MASKED_LOG2 = -1e30

C_F = 0
C_QD = C_F + F_WIDTH
C_KVD = C_QD + Q_LORA
C_GQ = C_KVD + KV_LORA
C_GK = C_GQ + GLA_QK
C_SMALL = C_GK + GLA_QK
C_GV = C_SMALL + LANES
C_GR = C_GV + GLA_WIDTH
C_GATE = C_GR + GLA_WIDTH
W1_COLS = C_GATE + N_BRANCH * D_MODEL
S_KR = 0
S_AF = 64
S_AB = 80


def _dot(a, b):
    return jnp.dot(a, b, preferred_element_type=F32)


def _dot_nt(a, b):
    return lax.dot_general(a, b, (((1,), (1,)), ((), ())), preferred_element_type=F32)


def _dot_tn(a, b):
    return lax.dot_general(a, b, (((0,), (0,)), ((), ())), preferred_element_type=F32)


def _split3(x):
    hi = x.astype(BF16)
    r1 = x - hi.astype(F32)
    mid = r1.astype(BF16)
    lo = (r1 - mid.astype(F32)).astype(BF16)
    return hi, mid, lo


def _dot_exact_lhs(m, x):
    hi, mid, lo = _split3(x)
    return _dot(m, hi) + _dot(m, mid) + _dot(m, lo)


def _rms(x):
    return x * lax.rsqrt(jnp.mean(x * x, axis=-1, keepdims=True) + NORM_EPS)


def _sigmoid(x):
    return 1.0 / (1.0 + jnp.exp(-x))


def _const_spec(shape):
    return pl.BlockSpec(shape, lambda *_: (0,) * len(shape))


def _params(sem):
    return pltpu.CompilerParams(dimension_semantics=sem, vmem_limit_bytes=VMEM_LIMIT_BYTES)


def _mod_kernel(c_ref, w_ref, b_ref, o_ref):
    c = c_ref[...]
    s = (c * _sigmoid(c)).astype(BF16)
    o_ref[0] = _dot(s, w_ref[0].astype(BF16)) + b_ref[0]


def _mod_call(cond, w_mod, b_mod):
    depth = w_mod.shape[0]
    n_col = w_mod.shape[2] // D_MODEL
    return pl.pallas_call(
        _mod_kernel,
        grid=(depth, n_col),
        in_specs=[
            pl.BlockSpec((MOD_ROWS, D_MODEL), lambda l, j: (0, 0)),
            pl.BlockSpec((1, D_MODEL, D_MODEL), lambda l, j: (l, 0, j)),
            pl.BlockSpec((1, 1, D_MODEL), lambda l, j: (l, 0, j)),
        ],
        out_specs=pl.BlockSpec((1, MOD_ROWS, D_MODEL), lambda l, j: (l, 0, j)),
        out_shape=jax.ShapeDtypeStruct((depth, MOD_ROWS, w_mod.shape[2]), F32),
        compiler_params=_params(("arbitrary", "arbitrary")),
        name="mod_proj",
    )(cond, w_mod, b_mod.reshape(depth, 1, -1))


def _inproj_kernel(*refs, rope, emit_ctx):
    it = iter(refs)
    x_ref, mod_ref, g1_ref, w1_ref, dft_ref, gq_ref, wqt_ref = (next(it) for _ in range(7))
    wqrt_ref = next(it) if rope else None
    gkv_ref, wk_ref, wvt_ref, wkr_ref, wa_ref, ba_ref = (next(it) for _ in range(6))
    if rope:
        cq_ref, sq_ref, ct_ref, st_ref = (next(it) for _ in range(4))
    fcs_ref, qt_ref, k_ref, vt_ref, qk_ref, gv_ref, la_ref, sg_ref, gate_ref = (next(it) for _ in range(9))
    if emit_ctx:
        ckv_ref, kr_ref = next(it), next(it)

    x = x_ref[0]
    mod = mod_ref[0]
    shift1, scale1 = mod[0:1], mod[1:2]
    h = _rms(x) * g1_ref[...] * (1.0 + scale1) + shift1
    hb = h.astype(BF16)

    def branch_gate(br):
        lo = C_GATE + br * D_MODEL
        zg = _dot(hb, w1_ref[:, lo:lo + D_MODEL])
        gate_ref[0, :, br * D_MODEL:(br + 1) * D_MODEL] = _sigmoid(zg).astype(BF16)

    z_qd = _dot(hb, w1_ref[:, C_QD:C_QD + Q_LORA])
    z_kg = _dot(hb, w1_ref[:, C_KVD:C_KVD + KV_LORA + GLA_QK])
    z_ks = _dot(hb, w1_ref[:, C_GK:C_GK + GLA_QK + LANES])
    kr2 = _dot(hb, wkr_ref[...])
    branch_gate(0)

    cq = (_rms(z_qd) * gq_ref[...]).astype(BF16)
    qa_t = _dot_nt(wqt_ref[...], cq)
    scale = (QK_NOPE + QK_ROPE) ** -0.5 * LOG2_E
    if rope:
        qb_t = _dot_nt(wqrt_ref[...], cq)
        cos_t, sin_t = ct_ref[...], st_ref[...]
    for hd in range(MLA_HEADS):
        sl = slice(hd * HEAD_PAD, (hd + 1) * HEAD_PAD)
        q_hd = (qa_t[sl] * cos_t + qb_t[sl] * sin_t) if rope else qa_t[sl]
        qt_ref[0, hd] = (q_hd * scale).astype(BF16)

    ckv = _rms(z_kg[:, 0:KV_LORA]) * gkv_ref[...]
    ckvb = ckv.astype(BF16)
    k_nope = _dot(ckvb, wk_ref[...])
    vt_ref[0] = _dot_nt(wvt_ref[...], ckvb).astype(BF16)
    k_rope = kr2[:, 0:HEAD_PAD]
    for hd in range(MLA_HEADS):
        sl = slice(hd * HEAD_PAD, (hd + 1) * HEAD_PAD)
        k_hd = k_nope[:, sl] + k_rope
        if rope:
            k_hd = k_hd * cq_ref[...] + kr2[:, HEAD_PAD:2 * HEAD_PAD] * sq_ref[...]
        k_ref[0, :, sl] = k_hd.astype(BF16)
    small = z_ks[:, GLA_QK:GLA_QK + LANES]
    if emit_ctx:
        ckv_ref[0] = ckv
        kr_ref[0] = small[:, S_KR:S_KR + QK_ROPE]
    branch_gate(1)

    zf = _dot(hb, w1_ref[:, C_F:C_F + F_WIDTH])

    qk_ref[0, :, 0:GLA_QK] = z_kg[:, KV_LORA:KV_LORA + GLA_QK] * (GLA_DK ** -0.5)
    qk_ref[0, :, GLA_QK:2 * GLA_QK] = z_ks[:, 0:GLA_QK]
    gv_ref[0] = _dot(hb, w1_ref[:, C_GV:C_GV + GLA_WIDTH])
    a_pre = _dot(small.astype(BF16), wa_ref[...]) + ba_ref[...]
    log_sig = jnp.minimum(a_pre, 0.0) - jnp.log(1.0 + jnp.exp(-jnp.abs(a_pre)))
    la_ref[0] = log_sig * (1.0 / GLA_TAU)
    zgr = _dot(hb, w1_ref[:, C_GR:C_GR + GLA_WIDTH])
    sg_ref[0] = zgr * _sigmoid(zgr)
    fcs_ref[0] = _dot(zf.astype(BF16), dft_ref[...]).astype(BF16)
    branch_gate(2)


def _inproj_call(x, mod, lw, tabs, *, rope, emit_ctx, tm):
    nb, seq, _ = x.shape
    nt = seq // tm
    mod_idx = (lambda b, t: (b, 0, 0)) if mod.shape[0] == nb else (lambda b, t: (0, 0, 0))
    tok = lambda w: pl.BlockSpec((1, tm, w), lambda b, t: (b, t, 0))

    ins = [x, mod, lw["g1"], lw["w1"], lw["dft64"], lw["gq"], lw["wqt"]]
    specs = [tok(D_MODEL), pl.BlockSpec((1, N_MOD, D_MODEL), mod_idx),
             _const_spec((1, D_MODEL)), _const_spec((D_MODEL, W1_COLS)),
             _const_spec((F_WIDTH, 2 * F_WIDTH)), _const_spec((1, Q_LORA)),
             _const_spec((MLA_HEADS * HEAD_PAD, Q_LORA))]
    if rope:
        ins.append(lw["wqrt"])
        specs.append(_const_spec((MLA_HEADS * HEAD_PAD, Q_LORA)))
    ins += [lw["gkv"], lw["wk"], lw["wvt"], lw["wkr"], lw["wa"], lw["ba"]]
    specs += [_const_spec((1, KV_LORA)), _const_spec((KV_LORA, MLA_HEADS * HEAD_PAD)),
              _const_spec((MLA_WIDTH, KV_LORA)), _const_spec((D_MODEL, 2 * HEAD_PAD)),
              _const_spec((LANES, 2 * GLA_QK)), _const_spec((1, 2 * GLA_QK))]
    if rope:
        ins += [tabs["cos_q"], tabs["sin_q"], tabs["cos_t"], tabs["sin_t"]]
        specs += [pl.BlockSpec((tm, HEAD_PAD), lambda b, t: (t, 0)),
                  pl.BlockSpec((tm, HEAD_PAD), lambda b, t: (t, 0)),
                  pl.BlockSpec((HEAD_PAD, tm), lambda b, t: (0, t)),
                  pl.BlockSpec((HEAD_PAD, tm), lambda b, t: (0, t))]

    def sds(w, dt):
        return jax.ShapeDtypeStruct((nb, seq, w), dt)

    out_shape = [sds(2 * F_WIDTH, BF16),
                 jax.ShapeDtypeStruct((nb, MLA_HEADS, HEAD_PAD, seq), BF16),
                 sds(MLA_HEADS * HEAD_PAD, BF16),
                 jax.ShapeDtypeStruct((nb, MLA_WIDTH, seq), BF16), sds(2 * GLA_QK, F32), sds(GLA_WIDTH, F32),
                 sds(2 * GLA_QK, F32), sds(GLA_WIDTH, F32), sds(N_BRANCH * D_MODEL, BF16)]
    out_specs = [tok(2 * F_WIDTH),
                 pl.BlockSpec((1, MLA_HEADS, HEAD_PAD, tm), lambda b, t: (b, 0, 0, t)),
                 tok(MLA_HEADS * HEAD_PAD),
                 pl.BlockSpec((1, MLA_WIDTH, tm), lambda b, t: (b, 0, t)),
                 tok(2 * GLA_QK), tok(GLA_WIDTH), tok(2 * GLA_QK), tok(GLA_WIDTH),
                 tok(N_BRANCH * D_MODEL)]
    if emit_ctx:
        out_shape += [sds(KV_LORA, F32), sds(QK_ROPE, F32)]
        out_specs += [tok(KV_LORA), tok(QK_ROPE)]

    return pl.pallas_call(
        functools.partial(_inproj_kernel, rope=rope, emit_ctx=emit_ctx),
        grid=(nb, nt), in_specs=specs, out_specs=out_specs, out_shape=out_shape,
        compiler_params=_params(("parallel", "parallel")),
        name="in_proj_lat" if rope else "in_proj_ctx",
    )(*ins)


def _kvcache_kernel(ckv_ref, kr_ref, wk_ref, wvt_ref, k_ref, vt_ref):
    ckvb = ckv_ref[0].astype(BF16)
    k_nope = _dot(ckvb, wk_ref[...])
    vt_ref[0] = _dot_nt(wvt_ref[...], ckvb).astype(BF16)
    k_rope = kr_ref[0]
    for hd in range(MLA_HEADS):
        sl = slice(hd * HEAD_PAD, (hd + 1) * HEAD_PAD)
        k_ref[0, :, sl] = (k_nope[:, sl] + k_rope).astype(BF16)


def _kvcache_call(ckv, kr_padded, lw):
    nb, past, _ = ckv.shape
    return pl.pallas_call(
        _kvcache_kernel,
        grid=(nb,),
        in_specs=[pl.BlockSpec((1, past, KV_LORA), lambda b: (b, 0, 0)),
                  pl.BlockSpec((1, past, HEAD_PAD), lambda b: (b, 0, 0)),
                  _const_spec((KV_LORA, MLA_HEADS * HEAD_PAD)), _const_spec((MLA_WIDTH, KV_LORA))],
        out_specs=[pl.BlockSpec((1, past, MLA_HEADS * HEAD_PAD), lambda b: (b, 0, 0)),
                   pl.BlockSpec((1, MLA_WIDTH, past), lambda b: (b, 0, 0))],
        out_shape=[jax.ShapeDtypeStruct((nb, past, MLA_HEADS * HEAD_PAD), BF16),
                   jax.ShapeDtypeStruct((nb, MLA_WIDTH, past), BF16)],
        compiler_params=_params(("parallel",)),
        name="kv_cache",
    )(ckv, kr_padded, lw["wk"], lw["wvt"])


def _fourier_kernel(c_ref, ns_ref, fcs_ref, o_ref, *, norm):
    fc = fcs_ref[0, :, 0:F_WIDTH]
    fs = fcs_ref[0, :, F_WIDTH:2 * F_WIDTH]
    o_ref[0] = ((_dot(c_ref[...], fc) + _dot(ns_ref[...], fs)) * norm).astype(BF16)


def _fourier_call(fcs, dft_c, dft_ns, *, tr):
    nb, seq, _ = fcs.shape
    norm = 1.0 / math.sqrt(seq * F_GROUP_DIM)
    return pl.pallas_call(
        functools.partial(_fourier_kernel, norm=norm),
        grid=(seq // tr, nb),
        in_specs=[pl.BlockSpec((tr, seq), lambda i, b: (i, 0)),
                  pl.BlockSpec((tr, seq), lambda i, b: (i, 0)),
                  pl.BlockSpec((1, seq, 2 * F_WIDTH), lambda i, b: (b, 0, 0))],
        out_specs=pl.BlockSpec((1, tr, F_WIDTH), lambda i, b: (b, i, 0)),
        out_shape=jax.ShapeDtypeStruct((nb, seq, F_WIDTH), BF16),
        compiler_params=_params(("parallel", "parallel")),
        name="fourier",
    )(dft_c, dft_ns, fcs)


def _attn_t_kernel(*refs, has_cache, ck, ru, hp):
    if has_cache:
        qt_ref, k_ref, vt_ref, kc_ref, vtc_ref, o_ref, s_scr, p_scr = refs
    else:
        qt_ref, k_ref, vt_ref, o_ref, s_scr, p_scr = refs
        kc_ref = vtc_ref = None

    chunks = []
    row = 0
    for k_r, vt_r in ((kc_ref, vtc_ref), (k_ref, vt_ref)):
        if k_r is None:
            continue
        for off in range(0, k_r.shape[1], ck):
            w = min(ck, k_r.shape[1] - off)
            chunks.append((k_r, vt_r, off, row, w))
            row += w

    sub = 8

    def fold(t, x, op):
        for r in range(0, x.shape[0], sub):
            piece = x[r:r + sub]
            t = piece if t is None else op(t, piece)
        return t

    def scores(t, chunk, m_t):
        j, cols = units[t]
        k_r, _, off, r0, w = chunk
        s = _dot(k_r[0, off:off + w, j * HEAD_PAD:(j + 1) * HEAD_PAD], qt_ref[0, j, :, cols])
        s_scr[t % ATTN_SLOTS, r0:r0 + w, :] = s
        return fold(m_t, s, jnp.maximum)

    def probs(t, chunk, m, l_t):
        _, _, _, r0, w = chunk
        p = jnp.exp2(s_scr[t % ATTN_SLOTS, r0:r0 + w, :] - m)
        p_scr[t % ATTN_SLOTS, r0:r0 + w, :] = p.astype(BF16)
        return fold(l_t, p, jnp.add)

    def weighted(t, chunk, acc):
        _, vt_r, off, r0, w = chunk
        pair = units[t][0] // 2
        pv = _dot(vt_r[0, pair * LANES:(pair + 1) * LANES, off:off + w], p_scr[t % ATTN_SLOTS, r0:r0 + w, :])
        return pv if acc is None else acc + pv

    tq = qt_ref.shape[3]
    units = [(j, slice(c0, c0 + ru)) for c0 in range(0, tq, ru) for j in range(hp)]
    n_u = len(units)
    m_t, l_t, acc, m = ([None] * n_u for _ in range(4))
    vrow = lax.broadcasted_iota(jnp.int32, (LANES, ru), 0)
    for t in range(n_u + 2):
        if 1 <= t <= n_u:
            m[t - 1] = jnp.max(m_t[t - 1], axis=0, keepdims=True)
        for chunk in chunks:
            if t < n_u:
                m_t[t] = scores(t, chunk, m_t[t])
            if 1 <= t <= n_u:
                l_t[t - 1] = probs(t - 1, chunk, m[t - 1], l_t[t - 1])
            if 2 <= t:
                acc[t - 2] = weighted(t - 2, chunk, acc[t - 2])
        u = t - 2
        if u >= 0 and units[u][0] % 2 == 1:
            j, cols = units[u]
            even, odd = (acc[v] / jnp.sum(l_t[v], axis=0, keepdims=True) for v in (u - 1, u))
            pair_t = jnp.where(vrow < V_DIM, even, odd)
            o_ref[0, cols, (j // 2) * LANES:(j // 2 + 1) * LANES] = pair_t.T.astype(BF16)


def _attn_t_call(qt, k, vt, cache, *, tq, ck, ru, hp):
    nb, _, _, seq = qt.shape
    has_cache = cache is not None
    n_keys = seq + (cache[0].shape[1] if has_cache else 0)
    assert hp % 2 == 0 and MLA_HEADS % hp == 0 and 2 * V_DIM == LANES
    val_w = hp * V_DIM
    ins = [qt, k, vt]
    specs = [pl.BlockSpec((1, hp, HEAD_PAD, tq), lambda b, p, i: (b, p, 0, i)),
             pl.BlockSpec((1, seq, hp * HEAD_PAD), lambda b, p, i: (b, 0, p)),
             pl.BlockSpec((1, val_w, seq), lambda b, p, i: (b, p, 0))]
    if has_cache:
        past = cache[0].shape[1]
        ins += list(cache)
        specs += [pl.BlockSpec((1, past, hp * HEAD_PAD), lambda b, p, i: (b, 0, p)),
                  pl.BlockSpec((1, val_w, past), lambda b, p, i: (b, p, 0))]
    return pl.pallas_call(
        functools.partial(_attn_t_kernel, has_cache=has_cache, ck=ck, ru=ru, hp=hp),
        grid=(nb, MLA_HEADS // hp, seq // tq),
        in_specs=specs,
        out_specs=pl.BlockSpec((1, tq, val_w), lambda b, p, i: (b, i, p)),
        out_shape=jax.ShapeDtypeStruct((nb, seq, MLA_WIDTH), BF16),
        scratch_shapes=[pltpu.VMEM((ATTN_SLOTS, n_keys, ru), F32),
                        pltpu.VMEM((ATTN_SLOTS, n_keys, ru), BF16)],
        compiler_params=_params(("parallel", "parallel", "parallel")),
        name="attention_lat" if has_cache else "attention_ctx",
    )(*ins)


def _gla_tile(la, q, k, v, st_ref, cum_ref, sel_ref, e2_ref, mask_ref, kmask_ref, vmask_ref, lmask_refs, *,
              reverse):
    cum = _dot_exact_lhs(cum_ref[...], la) * LOG2_E
    yield None

    tot = cum[0:1] if reverse else cum[GLA_TILE - 1:GLA_TILE]
    qe = (q * jnp.exp2(cum)).astype(BF16)
    ke = (k * jnp.exp2(tot - cum)).astype(BF16)
    jrow = lax.broadcasted_iota(jnp.int32, (GLA_DIAG, GLA_QK), 0)
    n_sub = GLA_TILE // GLA_SUB
    subs = [slice(n * GLA_SUB, (n + 1) * GLA_SUB) for n in range(n_sub)]

    st = st_ref[...]
    inter = _dot_nt(qe, st.astype(BF16))
    st_ref[...] = st * jnp.exp2(tot) + _dot_tn(v.astype(BF16), ke) * mask_ref[...]

    rect = []
    for hl in GLA_LEVELS:
        qs, ks, vs, q_rows = [], [], [], []
        for lo in range(0, GLA_TILE, 2 * hl):
            mid, hi = lo + hl, lo + 2 * hl
            if reverse:
                edge, qr, kr = cum[mid:mid + 1], slice(lo, mid), slice(mid, hi)
            else:
                edge, qr, kr = cum[mid - 1:mid], slice(mid, hi), slice(lo, mid)
            qs.append(q[qr] * jnp.exp2(cum[qr] - edge))
            ks.append(k[kr] * jnp.exp2(edge - cum[kr]))
            vs.append(v[kr])
            q_rows.append(qr)
        k_half = jnp.concatenate(ks, axis=0)
        k_bd = (jnp.concatenate([k_half] * GLA_HEADS, axis=0) * kmask_ref[...]).astype(BF16)
        sc = _dot_nt(jnp.concatenate(qs, axis=0).astype(BF16), k_bd)
        rect.append((sc, jnp.concatenate(vs, axis=0), q_rows))
    yield None

    dg = GLA_DIAG
    spread = []
    for r in subs:
        pieces = []
        for b0 in range(r.start, r.stop, dg):
            bq, kk = cum[b0:b0 + dg], k[b0:b0 + dg]
            for i in range(dg):
                keep = (jrow >= i) if reverse else (jrow <= i)
                d = jnp.where(keep, bq[i:i + 1] - bq, GLA_MASKED_LOG2)
                pieces.append(q[b0 + i:b0 + i + 1] * kk * jnp.exp2(d))
        wmat = jnp.concatenate(pieces, axis=0).astype(BF16)
        spread.append(_dot(wmat, e2_ref[...]))
    yield None

    intra = []
    for r, a in zip(subs, spread):
        av = jnp.concatenate([
            (a[n * dg * dg:(n + 1) * dg * dg].reshape(dg, dg, GLA_WIDTH)
             * v[r.start + n * dg:r.start + (n + 1) * dg][None]).reshape(dg * dg, GLA_WIDTH)
            for n in range(GLA_SUB // dg)], axis=0)
        intra.append(_dot(sel_ref[...], av.astype(BF16)))
    crosses = []
    for (sc, v_half, q_rows), lmask_ref in zip(rect, lmask_refs):
        p = sc if lmask_ref is None else sc * lmask_ref[...]
        v_bd = (jnp.concatenate([v_half] * GLA_HEADS, axis=0) * vmask_ref[...]).astype(BF16)
        crosses.append((_dot(p.astype(BF16), v_bd), q_rows))
    yield None

    diag = jnp.concatenate(intra, axis=0)
    outs = [inter[r0:r0 + dg] + diag[r0:r0 + dg] for r0 in range(0, GLA_TILE, dg)]
    for cross, q_rows in crosses:
        at = 0
        for qr in q_rows:
            for r0 in range(qr.start, qr.stop, dg):
                outs[r0 // dg] = outs[r0 // dg] + cross[at:at + dg]
                at += dg
    yield jnp.concatenate(outs, axis=0)


def _alternate(*stage_generators):
    last = [None] * len(stage_generators)
    live = list(range(len(stage_generators)))
    while live:
        for g in list(live):
            try:
                last[g] = next(stage_generators[g])
            except StopIteration:
                live.remove(g)
    return last


def _gla_kernel(qk_ref, gv_ref, la_ref, sg_ref, gn_ref, s0_ref, tril_ref, triu_ref, sel_ref,
                e2_ref, e3_ref, mask_ref, kmask_ref, vmask_ref, *rest):
    lmask_refs = (None,) + tuple(rest[:len(GLA_LEVELS) - 1])
    og_ref, sout_ref, of_scr, ob_scr, stf_scr, stb_scr = rest[len(GLA_LEVELS) - 1:]
    seq = qk_ref.shape[1]
    n_tiles = seq // GLA_TILE

    def tile_rows(t):
        return pl.ds(pl.multiple_of(t * GLA_TILE, GLA_TILE), GLA_TILE)

    def load(rows, lo):
        return (la_ref[0, rows, lo:lo + GLA_QK], qk_ref[0, rows, 0:GLA_QK],
                qk_ref[0, rows, GLA_QK:2 * GLA_QK], gv_ref[0, rows, :])

    stf_scr[...] = s0_ref[0, 0]
    stb_scr[...] = s0_ref[0, 1]

    def sweep(t, carry):
        rf = tile_rows(t)
        rb = tile_rows(n_tiles - 1 - t)
        shared = (sel_ref, e2_ref, mask_ref, kmask_ref, vmask_ref, lmask_refs)
        of_scr[rf, :], ob_scr[rb, :] = _alternate(
            _gla_tile(*load(rf, 0), stf_scr, tril_ref, *shared, reverse=False),
            _gla_tile(*load(rb, GLA_QK), stb_scr, triu_ref, *shared, reverse=True))
        return carry

    lax.fori_loop(0, n_tiles, sweep, 0)
    sout_ref[0, 0] = stf_scr[...]
    sout_ref[0, 1] = stb_scr[...]

    def finish(t, carry):
        rows = tile_rows(t)
        o = of_scr[rows, :] + ob_scr[rows, :]
        o2 = o * o
        hi = o2.astype(BF16)
        lo = (o2 - hi.astype(F32)).astype(BF16)
        ms = _dot(hi, e3_ref[...]) + _dot(lo, e3_ref[...])
        og = o * lax.rsqrt(ms + NORM_EPS) * gn_ref[...] * sg_ref[0, rows, :]
        og_ref[0, rows, :] = og.astype(BF16)
        return carry

    lax.fori_loop(0, n_tiles, finish, 0)


def _gla_call(qk, gv, la, sg, gnorm, s0t, consts):
    nb, seq, _ = qk.shape
    tok = lambda w: pl.BlockSpec((1, seq, w), lambda b: (b, 0, 0))
    st_spec = pl.BlockSpec((1, 2, GLA_WIDTH, GLA_QK), lambda b: (b, 0, 0, 0))
    half_tile = GLA_TILE // 2
    return pl.pallas_call(
        _gla_kernel,
        grid=(nb,),
        in_specs=[tok(2 * GLA_QK), tok(GLA_WIDTH), tok(2 * GLA_QK), tok(GLA_WIDTH),
                  _const_spec((1, GLA_WIDTH)), st_spec,
                  _const_spec((GLA_TILE, GLA_TILE)), _const_spec((GLA_TILE, GLA_TILE)),
                  _const_spec((GLA_SUB, GLA_PAIR_ROWS)), _const_spec((GLA_QK, GLA_WIDTH)),
                  _const_spec((GLA_WIDTH, GLA_WIDTH)), _const_spec((GLA_WIDTH, GLA_QK)),
                  _const_spec((GLA_HEADS * half_tile, GLA_QK)), _const_spec((GLA_HEADS * half_tile, GLA_WIDTH))]
                 + [_const_spec((half_tile, GLA_HEADS * half_tile))] * (len(GLA_LEVELS) - 1),
        out_specs=[tok(GLA_WIDTH), st_spec],
        out_shape=[jax.ShapeDtypeStruct((nb, seq, GLA_WIDTH), BF16),
                   jax.ShapeDtypeStruct((nb, 2, GLA_WIDTH, GLA_QK), F32)],
        scratch_shapes=[pltpu.VMEM((seq, GLA_WIDTH), F32), pltpu.VMEM((seq, GLA_WIDTH), F32),
                        pltpu.VMEM((GLA_WIDTH, GLA_QK), F32), pltpu.VMEM((GLA_WIDTH, GLA_QK), F32)],
        compiler_params=_params(("parallel",)),
        name="gla",
    )(qk, gv, la, sg, gnorm, s0t, consts["tril"], consts["triu"], consts["sel"],
      consts["e2"], consts["e3"], consts["mask"], consts["kmask"], consts["vmask"], *consts["lmasks"])


def _tail_kernel(x_ref, f_ref, at_ref, og_ref, gate_ref, mod_ref, wof_ref, wom_ref, wog_ref, wout_ref,
                 g2_ref, w1_ref, w2_ref, gf_ref, o_ref, *, final, ff_chunk):
    mod = mod_ref[0]
    gate1, shift2, scale2, gate2 = mod[2:3], mod[3:4], mod[4:5], mod[5:6]
    ya = _dot(f_ref[0], wof_ref[...])
    yb = _dot(at_ref[0], wom_ref[...])
    yc = _dot(og_ref[0], wog_ref[...])
    merged = (gate_ref[0, :, 0:D_MODEL].astype(F32) * ya
              + gate_ref[0, :, D_MODEL:2 * D_MODEL].astype(F32) * yb
              + gate_ref[0, :, 2 * D_MODEL:3 * D_MODEL].astype(F32) * yc)
    x1 = x_ref[0] + gate1 * _dot(merged.astype(BF16), wout_ref[...])
    h2 = (_rms(x1) * g2_ref[...] * (1.0 + scale2) + shift2).astype(BF16)
    acc = None
    for c0 in range(0, D_FF, ff_chunk):
        u = jnp.maximum(_dot(h2, w1_ref[:, c0:c0 + ff_chunk]), 0.0)
        part = _dot((u * u).astype(BF16), w2_ref[c0:c0 + ff_chunk, :])
        acc = part if acc is None else acc + part
    x2 = x1 + gate2 * acc
    if final:
        x2 = _rms(x2) * gf_ref[...]
    o_ref[0] = x2


def _tail_call(x, f, attn, og, gates, mod, lw, final_g, *, final, tm):
    nb, seq, _ = x.shape
    mod_idx = (lambda b, t: (b, 0, 0)) if mod.shape[0] == nb else (lambda b, t: (0, 0, 0))
    tok = lambda w: pl.BlockSpec((1, tm, w), lambda b, t: (b, t, 0))
    return pl.pallas_call(
        functools.partial(_tail_kernel, final=final, ff_chunk=1024),
        grid=(nb, seq // tm),
        in_specs=[tok(D_MODEL), tok(F_WIDTH), tok(MLA_WIDTH), tok(GLA_WIDTH), tok(N_BRANCH * D_MODEL),
                  pl.BlockSpec((1, N_MOD, D_MODEL), mod_idx),
                  _const_spec((F_WIDTH, D_MODEL)), _const_spec((MLA_WIDTH, D_MODEL)),
                  _const_spec((GLA_WIDTH, D_MODEL)), _const_spec((D_MODEL, D_MODEL)),
                  _const_spec((1, D_MODEL)), _const_spec((D_MODEL, D_FF)), _const_spec((D_FF, D_MODEL)),
                  _const_spec((1, D_MODEL))],
        out_specs=tok(D_MODEL),
        out_shape=jax.ShapeDtypeStruct((nb, seq, D_MODEL), F32),
        compiler_params=_params(("parallel", "parallel")),
        name="tail",
    )(x, f, attn, og, gates, mod, lw["wof"], lw["wom"], lw["wog"], lw["wout"], lw["g2"], lw["wff1"],
      lw["wff2"], final_g)


def _rot_partner(w):
    q = QK_ROPE // 4
    return jnp.concatenate([-w[..., q:2 * q], w[..., 0:q], -w[..., 3 * q:4 * q], w[..., 2 * q:3 * q]], axis=-1)


def _rope_tables(seq):
    half = QK_ROPE // 2
    pos = jnp.arange(seq)
    row = (pos // GRID_W).astype(F32)
    col = (pos % GRID_W).astype(F32)
    inv = ROPE_BASE ** (-jnp.arange(0, half, 2, dtype=F32) / half)
    ang = jnp.concatenate([row[:, None] * inv, row[:, None] * inv, col[:, None] * inv, col[:, None] * inv], axis=1)
    cos, sin = jnp.cos(ang), jnp.sin(ang)
    pad = HEAD_PAD - QK_NOPE - QK_ROPE
    cos_q = jnp.concatenate([jnp.ones((seq, QK_NOPE), F32), cos, jnp.zeros((seq, pad), F32)], axis=1)
    sin_q = jnp.concatenate([jnp.zeros((seq, QK_NOPE), F32), sin, jnp.zeros((seq, pad), F32)], axis=1)
    return dict(cos_q=cos_q, sin_q=sin_q, cos_t=cos_q.T, sin_t=sin_q.T)


def _dft_tables(seq):
    split = math.gcd(seq, GRID_W)
    pos = jnp.arange(seq, dtype=jnp.int32)
    w = 2.0 * math.pi / seq

    def trig(k, axis):
        ang = ((k[:, None] * pos[None, :]) % seq).astype(F32) * w
        return jnp.expand_dims(jnp.cos(ang), axis), jnp.expand_dims(jnp.sin(ang), axis)

    ca, sa = trig(jnp.arange(seq // split, dtype=jnp.int32) * split, 1)
    cb, sb = trig(jnp.arange(split, dtype=jnp.int32), 0)
    cos = (ca * cb - sa * sb).reshape(seq, seq)
    neg_sin = (-(sa * cb + ca * sb)).reshape(seq, seq)
    return cos.astype(BF16), neg_sin.astype(BF16)


def _dft64():
    k = jnp.arange(F_GROUP_DIM, dtype=jnp.int32)
    ang = ((k[:, None] * k[None, :]) % F_GROUP_DIM).astype(F32) * (2.0 * math.pi / F_GROUP_DIM)
    eye = jnp.eye(F_GROUPS, dtype=F32)
    return jnp.concatenate([jnp.kron(eye, jnp.cos(ang)), jnp.kron(eye, jnp.sin(ang))], axis=1).astype(BF16)


def _gla_consts():
    idx = np.arange(GLA_TILE)
    tril = idx[None, :] <= idx[:, None]
    triu = idx[None, :] >= idx[:, None]
    hq = np.arange(GLA_QK) // GLA_DK
    hv = np.arange(GLA_WIDTH) // GLA_DV
    e2 = hq[:, None] == hv[None, :]
    e3 = (hv[:, None] == hv[None, :]) / GLA_DV
    mask = hv[:, None] == hq[None, :]
    sel = np.arange(GLA_SUB)[:, None] == (np.arange(GLA_PAIR_ROWS) // GLA_DIAG)[None, :]
    half_tile = GLA_TILE // 2
    row_head = np.arange(GLA_HEADS * half_tile) // half_tile
    kmask = row_head[:, None] == hq[None, :]
    vmask = row_head[:, None] == hv[None, :]
    pos = np.arange(half_tile)
    key_pos = np.tile(pos, GLA_HEADS)
    lmasks = [jnp.asarray((pos[:, None] // hl) == (key_pos[None, :] // hl), F32) for hl in GLA_LEVELS[1:]]
    return dict(tril=jnp.asarray(tril, BF16), triu=jnp.asarray(triu, BF16),
                sel=jnp.asarray(sel, BF16),
                e2=jnp.asarray(e2, BF16), e3=jnp.asarray(e3, BF16), mask=jnp.asarray(mask, F32),
                kmask=jnp.asarray(kmask, F32), vmask=jnp.asarray(vmask, F32), lmasks=lmasks)


def _layer_weights(l, w_in, norm1_g, mla_q_norm_g, mla_kv_norm_g, w_q_up, w_kv_up, gla_wa2_f, gla_ba_f,
                   gla_wa2_b, gla_ba_b, gla_norm_g, w_o_fourier, w_o_mla, w_o_gla, w_out, norm2_g, w_ff1,
                   w_ff2, dft64):
    wi = w_in[l]
    o = np.cumsum([0, F_WIDTH, Q_LORA, KV_LORA, QK_ROPE, GLA_QK, GLA_QK, GLA_WIDTH, GLA_WIDTH,
                   GLA_GATE_RANK, GLA_GATE_RANK, N_BRANCH * D_MODEL])
    zf, zqd, zkvd, zkr, zgq, zgk, zgv, zgr, zaf, zab, zgate = (wi[:, int(o[i]):int(o[i + 1])] for i in range(11))
    small = jnp.concatenate([zkr, jnp.zeros((D_MODEL, S_AF - QK_ROPE), F32), zaf, zab,
                             jnp.zeros((D_MODEL, LANES - S_AB - GLA_GATE_RANK), F32)], axis=1)
    w1 = jnp.concatenate([zf, zqd, zkvd, zgq, zgk, small, zgv, zgr, zgate], axis=1).astype(BF16)
    lane_pad = lambda w: jnp.pad(w, ((0, 0), (QK_NOPE, HEAD_PAD - QK_NOPE - QK_ROPE)))
    wkr = jnp.concatenate([lane_pad(zkr), lane_pad(_rot_partner(zkr))], axis=1).astype(BF16)

    wq3 = w_q_up[l].reshape(Q_LORA, MLA_HEADS, QK_NOPE + QK_ROPE)
    pad = jnp.zeros((Q_LORA, MLA_HEADS, HEAD_PAD - QK_NOPE - QK_ROPE), F32)
    wqt = jnp.concatenate([wq3, pad], axis=2).reshape(Q_LORA, -1).T.astype(BF16)
    wqrt = jnp.concatenate([jnp.zeros((Q_LORA, MLA_HEADS, QK_NOPE), F32), _rot_partner(wq3[..., QK_NOPE:]), pad],
                           axis=2).reshape(Q_LORA, -1).T.astype(BF16)
    wkv3 = w_kv_up[l].reshape(KV_LORA, MLA_HEADS, QK_NOPE + V_DIM)
    wk = jnp.pad(wkv3[..., :QK_NOPE], ((0, 0), (0, 0), (0, HEAD_PAD - QK_NOPE))).reshape(KV_LORA, -1).astype(BF16)
    wvt = wkv3[..., QK_NOPE:].reshape(KV_LORA, -1).T.astype(BF16)

    wa = jnp.zeros((LANES, 2 * GLA_QK), F32)
    wa = wa.at[S_AF:S_AF + GLA_GATE_RANK, 0:GLA_QK].set(gla_wa2_f[l])
    wa = wa.at[S_AB:S_AB + GLA_GATE_RANK, GLA_QK:].set(gla_wa2_b[l]).astype(BF16)
    ba = jnp.concatenate([gla_ba_f[l], gla_ba_b[l]])[None]
    return dict(
        g1=norm1_g[l][None], w1=w1, dft64=dft64, gq=mla_q_norm_g[l][None], wqt=wqt, wqrt=wqrt,
        gkv=mla_kv_norm_g[l][None], wk=wk, wvt=wvt, wkr=wkr, wa=wa, ba=ba, gn=gla_norm_g[l][None],
        wof=w_o_fourier[l].astype(BF16), wom=w_o_mla[l].astype(BF16), wog=w_o_gla[l].astype(BF16),
        wout=w_out[l].astype(BF16), g2=norm2_g[l][None], wff1=w_ff1[l].astype(BF16),
        wff2=w_ff2[l].astype(BF16))


def _state_to_blockdiag_t(s):
    nb = s.shape[0]
    eye = jnp.eye(GLA_HEADS, dtype=s.dtype)
    st = jnp.einsum("bxhde,hg->bxhegd", s, eye)
    return st.reshape(nb, 2, GLA_WIDTH, GLA_QK)


def _blockdiag_t_to_state(st):
    nb = st.shape[0]
    s6 = st.reshape(nb, 2, GLA_HEADS, GLA_DV, GLA_HEADS, GLA_DK)
    diag = jnp.stack([s6[:, :, h, :, h, :] for h in range(GLA_HEADS)], axis=2)
    return jnp.swapaxes(diag, -1, -2)


def _pick_tile(seq, target):
    t = min(seq, target)
    assert seq % t == 0 and t % GLA_SUB == 0, (seq, t)
    return t


def _mixer_and_tail(x, mod, lw, tabs, dft, gla_c, cache, s0t, final_g, *, rope, final):
    nb, seq, _ = x.shape
    tm = _pick_tile(seq, 512)
    outs = _inproj_call(x, mod, lw, tabs, rope=rope, emit_ctx=not rope, tm=tm)
    fcs, qt, k, vt, qk, gv, la, sg, gates = outs[:9]
    f = _fourier_call(fcs, dft[0], dft[1], tr=_pick_tile(seq, 512))
    tq = _pick_tile(seq, 1024)
    hp = MLA_HEADS if seq <= 512 else 2
    attn = _attn_t_call(qt, k, vt, cache, tq=tq, ck=512, ru=min(tq, 256), hp=hp)
    og, st_out = _gla_call(qk, gv, la, sg, lw["gn"], s0t, gla_c)
    y = _tail_call(x, f, attn, og, gates, mod, lw, final_g, final=final, tm=tm)
    return y, outs[9:], st_out


def kernel(x_prompt, x_sample, c, cache_mla_ckv, cache_mla_krope, state_gla, c_ctx, w_mod, b_mod, norm1_g, w_in, mla_q_norm_g, mla_kv_norm_g, w_q_up, w_kv_up, gla_wa2_f, gla_ba_f, gla_wa2_b, gla_ba_b, gla_norm_g, w_o_fourier, w_o_mla, w_o_gla, w_out, norm2_g, w_ff1, w_ff2, final_norm_g):
    depth = w_mod.shape[0]
    n_ctx, seq_ctx, _ = x_prompt.shape
    n_lat, seq_lat, _ = x_sample.shape
    assert seq_ctx % GLA_TILE == 0 and seq_lat % GLA_TILE == 0 and seq_lat % GRID_W == 0
    assert 1 + n_lat <= MOD_ROWS

    cond = jnp.concatenate([c_ctx[None], c, jnp.zeros((MOD_ROWS - 1 - n_lat, D_MODEL), F32)], axis=0)
    mod_all = _mod_call(cond, w_mod, b_mod).reshape(depth, MOD_ROWS, N_MOD, D_MODEL)

    dft64 = _dft64()
    gla_c = _gla_consts()
    tabs = _rope_tables(seq_lat)
    dft_ctx = _dft_tables(seq_ctx)
    dft_lat = _dft_tables(seq_lat)
    final_g = final_norm_g[None]
    zero_state = jnp.zeros((n_ctx, 2, GLA_WIDTH, GLA_QK), F32)

    xp, xs = x_prompt, x_sample
    ckv_list, kr_list, gla_list = [], [], []
    for l in range(depth):
        lw = _layer_weights(l, w_in, norm1_g, mla_q_norm_g, mla_kv_norm_g, w_q_up, w_kv_up, gla_wa2_f,
                            gla_ba_f, gla_wa2_b, gla_ba_b, gla_norm_g, w_o_fourier, w_o_mla, w_o_gla,
                            w_out, norm2_g, w_ff1, w_ff2, dft64)
        final = l == depth - 1
        mod_ctx = mod_all[l, 0:1]
        mod_lat = mod_all[l, 1:1 + n_lat]

        xp, (ckv_c, kr_c), st_c = _mixer_and_tail(xp, mod_ctx, lw, None, dft_ctx, gla_c, None, zero_state,
                                                  final_g, rope=False, final=final)
        ckv_list.append(ckv_c)
        kr_list.append(kr_c)
        gla_list.append(_blockdiag_t_to_state(st_c))

        kr_padded = jnp.pad(cache_mla_krope[:, l], ((0, 0), (0, 0), (QK_NOPE, HEAD_PAD - QK_NOPE - QK_ROPE)))
        cache = _kvcache_call(cache_mla_ckv[:, l], kr_padded, lw)
        s0t = _state_to_blockdiag_t(state_gla[:, l].astype(F32))
        xs, _, _ = _mixer_and_tail(xs, mod_lat, lw, tabs, dft_lat, gla_c, cache, s0t, final_g,
                                   rope=True, final=final)

    return (xp, xs, jnp.stack(ckv_list, axis=1), jnp.stack(kr_list, axis=1), jnp.stack(gla_list, axis=1))
```

```python
import functools
import math

import numpy as np
import jax
import jax.numpy as jnp
from jax import lax
from jax.experimental import pallas as pl
from jax.experimental.pallas import tpu as pltpu

F32 = jnp.float32
BF16 = jnp.bfloat16

D_MODEL = 1024
GRID_W = 64
F_GROUPS = 4
F_GROUP_DIM = 64
F_WIDTH = F_GROUPS * F_GROUP_DIM
MLA_HEADS = 8
QK_NOPE = 64
QK_ROPE = 32
V_DIM = 64
Q_LORA = 256
KV_LORA = 128
MLA_WIDTH = MLA_HEADS * V_DIM
GLA_HEADS = 4
GLA_DK = 32
GLA_DV = 64
GLA_WIDTH = GLA_HEADS * GLA_DV
GLA_QK = GLA_HEADS * GLA_DK
GLA_GATE_RANK = 16
GLA_TAU = 16.0
D_FF = 4 * D_MODEL
N_BRANCH = 3
ROPE_BASE = 10000.0
NORM_EPS = 1e-6
N_MOD = 6
LOG2_E = math.log2(math.e)

LANES = 128
VMEM_LIMIT_BYTES = 56 * 1024 * 1024

HEAD_PAD = LANES
ATTN_SLOTS = 3
GLA_SUB = 16
GLA_TILE = 128
GLA_DIAG = 8
GLA_PAIR_ROWS = GLA_SUB * GLA_DIAG
GLA_LEVELS = tuple(GLA_TILE >> s for s in range(1, (GLA_TILE // GLA_DIAG).bit_length()))
MOD_ROWS = 16
GLA_MASKED_LOG2 = -1e30

C_F = 0
C_QD = C_F + F_WIDTH
C_KVD = C_QD + Q_LORA
C_GQ = C_KVD + KV_LORA
C_GK = C_GQ + GLA_QK
C_SMALL = C_GK + GLA_QK
C_GV = C_SMALL + LANES
C_GR = C_GV + GLA_WIDTH
C_GATE = C_GR + GLA_WIDTH
W1_COLS = C_GATE + N_BRANCH * D_MODEL
S_KR = 0
S_AF = 64
S_AB = 80


def _dot(a, b):
    return jnp.dot(a, b, preferred_element_type=F32)


def _dot_nt(a, b):
    return lax.dot_general(a, b, (((1,), (1,)), ((), ())), preferred_element_type=F32)


def _dot_tn(a, b):
    return lax.dot_general(a, b, (((0,), (0,)), ((), ())), preferred_element_type=F32)


def _split3(x):
    hi = x.astype(BF16)
    r1 = x - hi.astype(F32)
    mid = r1.astype(BF16)
    lo = (r1 - mid.astype(F32)).astype(BF16)
    return hi, mid, lo


def _dot_exact_lhs(m, x):
    hi, mid, lo = _split3(x)
    return _dot(m, hi) + _dot(m, mid) + _dot(m, lo)


def _rms(x):
    return x * lax.rsqrt(jnp.mean(x * x, axis=-1, keepdims=True) + NORM_EPS)


def _sigmoid(x):
    return 1.0 / (1.0 + jnp.exp(-x))


def _const_spec(shape):
    return pl.BlockSpec(shape, lambda *_: (0,) * len(shape))


def _params(sem):
    return pltpu.CompilerParams(dimension_semantics=sem, vmem_limit_bytes=VMEM_LIMIT_BYTES)


def _mod_kernel(c_ref, w_ref, b_ref, o_ref):
    c = c_ref[...]
    s = (c * _sigmoid(c)).astype(BF16)
    o_ref[0] = _dot(s, w_ref[0].astype(BF16)) + b_ref[0]


def _mod_call(cond, w_mod, b_mod):
    depth = w_mod.shape[0]
    n_col = w_mod.shape[2] // D_MODEL
    return pl.pallas_call(
        _mod_kernel,
        grid=(depth, n_col),
        in_specs=[
            pl.BlockSpec((MOD_ROWS, D_MODEL), lambda l, j: (0, 0)),
            pl.BlockSpec((1, D_MODEL, D_MODEL), lambda l, j: (l, 0, j)),
            pl.BlockSpec((1, 1, D_MODEL), lambda l, j: (l, 0, j)),
        ],
        out_specs=pl.BlockSpec((1, MOD_ROWS, D_MODEL), lambda l, j: (l, 0, j)),
        out_shape=jax.ShapeDtypeStruct((depth, MOD_ROWS, w_mod.shape[2]), F32),
        compiler_params=_params(("arbitrary", "arbitrary")),
        name="mod_proj",
    )(cond, w_mod, b_mod.reshape(depth, 1, -1))


def _inproj_kernel(*refs, rope, emit_ctx):
    it = iter(refs)
    x_ref, mod_ref, g1_ref, w1_ref, dft_ref, gq_ref, wqt_ref = (next(it) for _ in range(7))
    wqrt_ref = next(it) if rope else None
    gkv_ref, wk_ref, wvt_ref, wkr_ref, wa_ref, ba_ref = (next(it) for _ in range(6))
    if rope:
        cq_ref, sq_ref, ct_ref, st_ref = (next(it) for _ in range(4))
    fcs_ref, qt_ref, k_ref, vt_ref, qk_ref, gv_ref, la_ref, sg_ref, gate_ref = (next(it) for _ in range(9))
    if emit_ctx:
        ckv_ref, kr_ref = next(it), next(it)

    x = x_ref[0]
    mod = mod_ref[0]
    shift1, scale1 = mod[0:1], mod[1:2]
    h = _rms(x) * g1_ref[...] * (1.0 + scale1) + shift1
    hb = h.astype(BF16)

    def branch_gate(br):
        lo = C_GATE + br * D_MODEL
        zg = _dot(hb, w1_ref[:, lo:lo + D_MODEL])
        gate_ref[0, :, br * D_MODEL:(br + 1) * D_MODEL] = _sigmoid(zg).astype(BF16)

    z_qd = _dot(hb, w1_ref[:, C_QD:C_QD + Q_LORA])
    z_kg = _dot(hb, w1_ref[:, C_KVD:C_KVD + KV_LORA + GLA_QK])
    z_ks = _dot(hb, w1_ref[:, C_GK:C_GK + GLA_QK + LANES])
    kr2 = _dot(hb, wkr_ref[...])
    branch_gate(0)

    cq = (_rms(z_qd) * gq_ref[...]).astype(BF16)
    qa_t = _dot_nt(wqt_ref[...], cq)
    scale = (QK_NOPE + QK_ROPE) ** -0.5 * LOG2_E
    if rope:
        qb_t = _dot_nt(wqrt_ref[...], cq)
        cos_t, sin_t = ct_ref[...], st_ref[...]
    for hd in range(MLA_HEADS):
        sl = slice(hd * HEAD_PAD, (hd + 1) * HEAD_PAD)
        q_hd = (qa_t[sl] * cos_t + qb_t[sl] * sin_t) if rope else qa_t[sl]
        qt_ref[0, hd] = (q_hd * scale).astype(BF16)

    ckv = _rms(z_kg[:, 0:KV_LORA]) * gkv_ref[...]
    ckvb = ckv.astype(BF16)
    k_nope = _dot(ckvb, wk_ref[...])
    vt_ref[0] = _dot_nt(wvt_ref[...], ckvb).astype(BF16)
    k_rope = kr2[:, 0:HEAD_PAD]
    for hd in range(MLA_HEADS):
        sl = slice(hd * HEAD_PAD, (hd + 1) * HEAD_PAD)
        k_hd = k_nope[:, sl] + k_rope
        if rope:
            k_hd = k_hd * cq_ref[...] + kr2[:, HEAD_PAD:2 * HEAD_PAD] * sq_ref[...]
        k_ref[0, :, sl] = k_hd.astype(BF16)
    small = z_ks[:, GLA_QK:GLA_QK + LANES]
    if emit_ctx:
        ckv_ref[0] = ckv
        kr_ref[0] = small[:, S_KR:S_KR + QK_ROPE]
    branch_gate(1)

    zf = _dot(hb, w1_ref[:, C_F:C_F + F_WIDTH])

    qk_ref[0, :, 0:GLA_QK] = z_kg[:, KV_LORA:KV_LORA + GLA_QK] * (GLA_DK ** -0.5)
    qk_ref[0, :, GLA_QK:2 * GLA_QK] = z_ks[:, 0:GLA_QK]
    gv_ref[0] = _dot(hb, w1_ref[:, C_GV:C_GV + GLA_WIDTH])
    a_pre = _dot(small.astype(BF16), wa_ref[...]) + ba_ref[...]
    log_sig = jnp.minimum(a_pre, 0.0) - jnp.log(1.0 + jnp.exp(-jnp.abs(a_pre)))
    la_ref[0] = log_sig * (1.0 / GLA_TAU)
    zgr = _dot(hb, w1_ref[:, C_GR:C_GR + GLA_WIDTH])
    sg_ref[0] = zgr * _sigmoid(zgr)
    fcs_ref[0] = _dot(zf.astype(BF16), dft_ref[...]).astype(BF16)
    branch_gate(2)


def _inproj_call(x, mod, lw, tabs, *, rope, emit_ctx, tm):
    nb, seq, _ = x.shape
    nt = seq // tm
    mod_idx = (lambda b, t: (b, 0, 0)) if mod.shape[0] == nb else (lambda b, t: (0, 0, 0))
    tok = lambda w: pl.BlockSpec((1, tm, w), lambda b, t: (b, t, 0))

    ins = [x, mod, lw["g1"], lw["w1"], lw["dft64"], lw["gq"], lw["wqt"]]
    specs = [tok(D_MODEL), pl.BlockSpec((1, N_MOD, D_MODEL), mod_idx),
             _const_spec((1, D_MODEL)), _const_spec((D_MODEL, W1_COLS)),
             _const_spec((F_WIDTH, 2 * F_WIDTH)), _const_spec((1, Q_LORA)),
             _const_spec((MLA_HEADS * HEAD_PAD, Q_LORA))]
    if rope:
        ins.append(lw["wqrt"])
        specs.append(_const_spec((MLA_HEADS * HEAD_PAD, Q_LORA)))
    ins += [lw["gkv"], lw["wk"], lw["wvt"], lw["wkr"], lw["wa"], lw["ba"]]
    specs += [_const_spec((1, KV_LORA)), _const_spec((KV_LORA, MLA_HEADS * HEAD_PAD)),
              _const_spec((MLA_WIDTH, KV_LORA)), _const_spec((D_MODEL, 2 * HEAD_PAD)),
              _const_spec((LANES, 2 * GLA_QK)), _const_spec((1, 2 * GLA_QK))]
    if rope:
        ins += [tabs["cos_q"], tabs["sin_q"], tabs["cos_t"], tabs["sin_t"]]
        specs += [pl.BlockSpec((tm, HEAD_PAD), lambda b, t: (t, 0)),
                  pl.BlockSpec((tm, HEAD_PAD), lambda b, t: (t, 0)),
                  pl.BlockSpec((HEAD_PAD, tm), lambda b, t: (0, t)),
                  pl.BlockSpec((HEAD_PAD, tm), lambda b, t: (0, t))]

    def sds(w, dt):
        return jax.ShapeDtypeStruct((nb, seq, w), dt)

    out_shape = [sds(2 * F_WIDTH, BF16),
                 jax.ShapeDtypeStruct((nb, MLA_HEADS, HEAD_PAD, seq), BF16),
                 sds(MLA_HEADS * HEAD_PAD, BF16),
                 jax.ShapeDtypeStruct((nb, MLA_WIDTH, seq), BF16), sds(2 * GLA_QK, F32), sds(GLA_WIDTH, F32),
                 sds(2 * GLA_QK, F32), sds(GLA_WIDTH, F32), sds(N_BRANCH * D_MODEL, BF16)]
    out_specs = [tok(2 * F_WIDTH),
                 pl.BlockSpec((1, MLA_HEADS, HEAD_PAD, tm), lambda b, t: (b, 0, 0, t)),
                 tok(MLA_HEADS * HEAD_PAD),
                 pl.BlockSpec((1, MLA_WIDTH, tm), lambda b, t: (b, 0, t)),
                 tok(2 * GLA_QK), tok(GLA_WIDTH), tok(2 * GLA_QK), tok(GLA_WIDTH),
                 tok(N_BRANCH * D_MODEL)]
    if emit_ctx:
        out_shape += [sds(KV_LORA, F32), sds(QK_ROPE, F32)]
        out_specs += [tok(KV_LORA), tok(QK_ROPE)]

    return pl.pallas_call(
        functools.partial(_inproj_kernel, rope=rope, emit_ctx=emit_ctx),
        grid=(nb, nt), in_specs=specs, out_specs=out_specs, out_shape=out_shape,
        compiler_params=_params(("parallel", "parallel")),
        name="in_proj_lat" if rope else "in_proj_ctx",
    )(*ins)


def _kvcache_kernel(ckv_ref, kr_ref, wk_ref, wvt_ref, k_ref, vt_ref):
    ckvb = ckv_ref[0].astype(BF16)
    k_nope = _dot(ckvb, wk_ref[...])
    vt_ref[0] = _dot_nt(wvt_ref[...], ckvb).astype(BF16)
    k_rope = kr_ref[0]
    for hd in range(MLA_HEADS):
        sl = slice(hd * HEAD_PAD, (hd + 1) * HEAD_PAD)
        k_ref[0, :, sl] = (k_nope[:, sl] + k_rope).astype(BF16)


def _kvcache_call(ckv, kr_padded, lw):
    nb, past, _ = ckv.shape
    return pl.pallas_call(
        _kvcache_kernel,
        grid=(nb,),
        in_specs=[pl.BlockSpec((1, past, KV_LORA), lambda b: (b, 0, 0)),
                  pl.BlockSpec((1, past, HEAD_PAD), lambda b: (b, 0, 0)),
                  _const_spec((KV_LORA, MLA_HEADS * HEAD_PAD)), _const_spec((MLA_WIDTH, KV_LORA))],
        out_specs=[pl.BlockSpec((1, past, MLA_HEADS * HEAD_PAD), lambda b: (b, 0, 0)),
                   pl.BlockSpec((1, MLA_WIDTH, past), lambda b: (b, 0, 0))],
        out_shape=[jax.ShapeDtypeStruct((nb, past, MLA_HEADS * HEAD_PAD), BF16),
                   jax.ShapeDtypeStruct((nb, MLA_WIDTH, past), BF16)],
        compiler_params=_params(("parallel",)),
        name="kv_cache",
    )(ckv, kr_padded, lw["wk"], lw["wvt"])


def _fourier_kernel(c_ref, ns_ref, fcs_ref, o_ref, *, norm):
    fc = fcs_ref[0, :, 0:F_WIDTH]
    fs = fcs_ref[0, :, F_WIDTH:2 * F_WIDTH]
    o_ref[0] = ((_dot(c_ref[...], fc) + _dot(ns_ref[...], fs)) * norm).astype(BF16)


def _fourier_call(fcs, dft_c, dft_ns, *, tr):
    nb, seq, _ = fcs.shape
    norm = 1.0 / math.sqrt(seq * F_GROUP_DIM)
    return pl.pallas_call(
        functools.partial(_fourier_kernel, norm=norm),
        grid=(seq // tr, nb),
        in_specs=[pl.BlockSpec((tr, seq), lambda i, b: (i, 0)),
                  pl.BlockSpec((tr, seq), lambda i, b: (i, 0)),
                  pl.BlockSpec((1, seq, 2 * F_WIDTH), lambda i, b: (b, 0, 0))],
        out_specs=pl.BlockSpec((1, tr, F_WIDTH), lambda i, b: (b, i, 0)),
        out_shape=jax.ShapeDtypeStruct((nb, seq, F_WIDTH), BF16),
        compiler_params=_params(("parallel", "parallel")),
        name="fourier",
    )(dft_c, dft_ns, fcs)


def _attn_t_kernel(*refs, has_cache, ck, ru, hp):
    if has_cache:
        qt_ref, k_ref, vt_ref, kc_ref, vtc_ref, o_ref, s_scr, p_scr = refs
    else:
        qt_ref, k_ref, vt_ref, o_ref, s_scr, p_scr = refs
        kc_ref = vtc_ref = None

    chunks = []
    row = 0
    for k_r, vt_r in ((kc_ref, vtc_ref), (k_ref, vt_ref)):
        if k_r is None:
            continue
        for off in range(0, k_r.shape[1], ck):
            w = min(ck, k_r.shape[1] - off)
            chunks.append((k_r, vt_r, off, row, w))
            row += w

    sub = 8

    def fold(t, x, op):
        for r in range(0, x.shape[0], sub):
            piece = x[r:r + sub]
            t = piece if t is None else op(t, piece)
        return t

    def scores(t, chunk, m_t):
        j, cols = units[t]
        k_r, _, off, r0, w = chunk
        s = _dot(k_r[0, off:off + w, j * HEAD_PAD:(j + 1) * HEAD_PAD], qt_ref[0, j, :, cols])
        s_scr[t % ATTN_SLOTS, r0:r0 + w, :] = s
        return fold(m_t, s, jnp.maximum)

    def probs(t, chunk, m, l_t):
        _, _, _, r0, w = chunk
        p = jnp.exp2(s_scr[t % ATTN_SLOTS, r0:r0 + w, :] - m)
        p_scr[t % ATTN_SLOTS, r0:r0 + w, :] = p.astype(BF16)
        return fold(l_t, p, jnp.add)

    def weighted(t, chunk, acc):
        _, vt_r, off, r0, w = chunk
        pair = units[t][0] // 2
        pv = _dot(vt_r[0, pair * LANES:(pair + 1) * LANES, off:off + w], p_scr[t % ATTN_SLOTS, r0:r0 + w, :])
        return pv if acc is None else acc + pv

    tq = qt_ref.shape[3]
    units = [(j, slice(c0, c0 + ru)) for c0 in range(0, tq, ru) for j in range(hp)]
    n_u = len(units)
    m_t, l_t, acc, m = ([None] * n_u for _ in range(4))
    vrow = lax.broadcasted_iota(jnp.int32, (LANES, ru), 0)
    for t in range(n_u + 2):
        if 1 <= t <= n_u:
            m[t - 1] = jnp.max(m_t[t - 1], axis=0, keepdims=True)
        for chunk in chunks:
            if t < n_u:
                m_t[t] = scores(t, chunk, m_t[t])
            if 1 <= t <= n_u:
                l_t[t - 1] = probs(t - 1, chunk, m[t - 1], l_t[t - 1])
            if 2 <= t:
                acc[t - 2] = weighted(t - 2, chunk, acc[t - 2])
        u = t - 2
        if u >= 0 and units[u][0] % 2 == 1:
            j, cols = units[u]
            even, odd = (acc[v] / jnp.sum(l_t[v], axis=0, keepdims=True) for v in (u - 1, u))
            pair_t = jnp.where(vrow < V_DIM, even, odd)
            o_ref[0, cols, (j // 2) * LANES:(j // 2 + 1) * LANES] = pair_t.T.astype(BF16)


def _attn_t_call(qt, k, vt, cache, *, tq, ck, ru, hp):
    nb, _, _, seq = qt.shape
    has_cache = cache is not None
    n_keys = seq + (cache[0].shape[1] if has_cache else 0)
    assert hp % 2 == 0 and MLA_HEADS % hp == 0 and 2 * V_DIM == LANES
    val_w = hp * V_DIM
    ins = [qt, k, vt]
    specs = [pl.BlockSpec((1, hp, HEAD_PAD, tq), lambda b, p, i: (b, p, 0, i)),
             pl.BlockSpec((1, seq, hp * HEAD_PAD), lambda b, p, i: (b, 0, p)),
             pl.BlockSpec((1, val_w, seq), lambda b, p, i: (b, p, 0))]
    if has_cache:
        past = cache[0].shape[1]
        ins += list(cache)
        specs += [pl.BlockSpec((1, past, hp * HEAD_PAD), lambda b, p, i: (b, 0, p)),
                  pl.BlockSpec((1, val_w, past), lambda b, p, i: (b, p, 0))]
    return pl.pallas_call(
        functools.partial(_attn_t_kernel, has_cache=has_cache, ck=ck, ru=ru, hp=hp),
        grid=(nb, MLA_HEADS // hp, seq // tq),
        in_specs=specs,
        out_specs=pl.BlockSpec((1, tq, val_w), lambda b, p, i: (b, i, p)),
        out_shape=jax.ShapeDtypeStruct((nb, seq, MLA_WIDTH), BF16),
        scratch_shapes=[pltpu.VMEM((ATTN_SLOTS, n_keys, ru), F32),
                        pltpu.VMEM((ATTN_SLOTS, n_keys, ru), BF16)],
        compiler_params=_params(("parallel", "parallel", "parallel")),
        name="attention_lat" if has_cache else "attention_ctx",
    )(*ins)


def _gla_tile(la, q, k, v, st_ref, cum_ref, sel_ref, e2_ref, mask_ref, kmask_ref, vmask_ref, lmask_refs, *,
              reverse):
    cum = _dot_exact_lhs(cum_ref[...], la) * LOG2_E
    yield None

    tot = cum[0:1] if reverse else cum[GLA_TILE - 1:GLA_TILE]
    qe = (q * jnp.exp2(cum)).astype(BF16)
    ke = (k * jnp.exp2(tot - cum)).astype(BF16)
    jrow = lax.broadcasted_iota(jnp.int32, (GLA_DIAG, GLA_QK), 0)
    n_sub = GLA_TILE // GLA_SUB
    subs = [slice(n * GLA_SUB, (n + 1) * GLA_SUB) for n in range(n_sub)]

    st = st_ref[...]
    inter = _dot_nt(qe, st.astype(BF16))
    st_ref[...] = st * jnp.exp2(tot) + _dot_tn(v.astype(BF16), ke) * mask_ref[...]

    rect = []
    for hl in GLA_LEVELS:
        qs, ks, vs, q_rows = [], [], [], []
        for lo in range(0, GLA_TILE, 2 * hl):
            mid, hi = lo + hl, lo + 2 * hl
            if reverse:
                edge, qr, kr = cum[mid:mid + 1], slice(lo, mid), slice(mid, hi)
            else:
                edge, qr, kr = cum[mid - 1:mid], slice(mid, hi), slice(lo, mid)
            qs.append(q[qr] * jnp.exp2(cum[qr] - edge))
            ks.append(k[kr] * jnp.exp2(edge - cum[kr]))
            vs.append(v[kr])
            q_rows.append(qr)
        k_half = jnp.concatenate(ks, axis=0)
        k_bd = (jnp.concatenate([k_half] * GLA_HEADS, axis=0) * kmask_ref[...]).astype(BF16)
        sc = _dot_nt(jnp.concatenate(qs, axis=0).astype(BF16), k_bd)
        rect.append((sc, jnp.concatenate(vs, axis=0), q_rows))
    yield None

    dg = GLA_DIAG
    spread = []
    for r in subs:
        pieces = []
        for b0 in range(r.start, r.stop, dg):
            bq, kk = cum[b0:b0 + dg], k[b0:b0 + dg]
            for i in range(dg):
                keep = (jrow >= i) if reverse else (jrow <= i)
                d = jnp.where(keep, bq[i:i + 1] - bq, GLA_MASKED_LOG2)
                pieces.append(q[b0 + i:b0 + i + 1] * kk * jnp.exp2(d))
        wmat = jnp.concatenate(pieces, axis=0).astype(BF16)
        spread.append(_dot(wmat, e2_ref[...]))
    yield None

    intra = []
    for r, a in zip(subs, spread):
        av = jnp.concatenate([
            (a[n * dg * dg:(n + 1) * dg * dg].reshape(dg, dg, GLA_WIDTH)
             * v[r.start + n * dg:r.start + (n + 1) * dg][None]).reshape(dg * dg, GLA_WIDTH)
            for n in range(GLA_SUB // dg)], axis=0)
        intra.append(_dot(sel_ref[...], av.astype(BF16)))
    crosses = []
    for (sc, v_half, q_rows), lmask_ref in zip(rect, lmask_refs):
        p = sc if lmask_ref is None else sc * lmask_ref[...]
        v_bd = (jnp.concatenate([v_half] * GLA_HEADS, axis=0) * vmask_ref[...]).astype(BF16)
        crosses.append((_dot(p.astype(BF16), v_bd), q_rows))
    yield None

    diag = jnp.concatenate(intra, axis=0)
    outs = [inter[r0:r0 + dg] + diag[r0:r0 + dg] for r0 in range(0, GLA_TILE, dg)]
    for cross, q_rows in crosses:
        at = 0
        for qr in q_rows:
            for r0 in range(qr.start, qr.stop, dg):
                outs[r0 // dg] = outs[r0 // dg] + cross[at:at + dg]
                at += dg
    yield jnp.concatenate(outs, axis=0)


def _alternate(*stage_generators):
    last = [None] * len(stage_generators)
    live = list(range(len(stage_generators)))
    while live:
        for g in list(live):
            try:
                last[g] = next(stage_generators[g])
            except StopIteration:
                live.remove(g)
    return last


def _gla_kernel(qk_ref, gv_ref, la_ref, sg_ref, gn_ref, s0_ref, tril_ref, triu_ref, sel_ref,
                e2_ref, e3_ref, mask_ref, kmask_ref, vmask_ref, *rest):
    lmask_refs = (None,) + tuple(rest[:len(GLA_LEVELS) - 1])
    og_ref, sout_ref, of_scr, ob_scr, stf_scr, stb_scr = rest[len(GLA_LEVELS) - 1:]
    seq = qk_ref.shape[1]
    n_tiles = seq // GLA_TILE

    def tile_rows(t):
        return pl.ds(pl.multiple_of(t * GLA_TILE, GLA_TILE), GLA_TILE)

    def load(rows, lo):
        return (la_ref[0, rows, lo:lo + GLA_QK], qk_ref[0, rows, 0:GLA_QK],
                qk_ref[0, rows, GLA_QK:2 * GLA_QK], gv_ref[0, rows, :])

    stf_scr[...] = s0_ref[0, 0]
    stb_scr[...] = s0_ref[0, 1]

    def sweep(t, finish_previous):
        rf = tile_rows(t)
        rb = tile_rows(n_tiles - 1 - t)
        shared = (sel_ref, e2_ref, mask_ref, kmask_ref, vmask_ref, lmask_refs)
        gens = [_gla_tile(*load(rf, 0), stf_scr, tril_ref, *shared, reverse=False),
                _gla_tile(*load(rb, GLA_QK), stb_scr, triu_ref, *shared, reverse=True)]
        if finish_previous:
            gens.append(finish(t - 1))
        of_scr[rf, :], ob_scr[rb, :] = _alternate(*gens)[:2]

    def finish(u):
        for t in (u, n_tiles - 1 - u):
            rows = tile_rows(t)
            o = of_scr[rows, :] + ob_scr[rows, :]
            o2 = o * o
            hi = o2.astype(BF16)
            lo = (o2 - hi.astype(F32)).astype(BF16)
            ms = _dot(hi, e3_ref[...]) + _dot(lo, e3_ref[...])
            yield None
            og = o * lax.rsqrt(ms + NORM_EPS) * gn_ref[...] * sg_ref[0, rows, :]
            og_ref[0, rows, :] = og.astype(BF16)
            yield None

    first_done = n_tiles // 2

    def loop(lo, hi, finish_previous):
        def body(t, carry):
            sweep(t, finish_previous)
            return carry
        lax.fori_loop(lo, hi, body, 0)

    loop(0, first_done + 1, False)
    loop(first_done + 1, n_tiles, True)
    sout_ref[0, 0] = stf_scr[...]
    sout_ref[0, 1] = stb_scr[...]
    for _ in finish(n_tiles - 1):
        pass


def _gla_call(qk, gv, la, sg, gnorm, s0t, consts):
    nb, seq, _ = qk.shape
    tok = lambda w: pl.BlockSpec((1, seq, w), lambda b: (b, 0, 0))
    st_spec = pl.BlockSpec((1, 2, GLA_WIDTH, GLA_QK), lambda b: (b, 0, 0, 0))
    half_tile = GLA_TILE // 2
    return pl.pallas_call(
        _gla_kernel,
        grid=(nb,),
        in_specs=[tok(2 * GLA_QK), tok(GLA_WIDTH), tok(2 * GLA_QK), tok(GLA_WIDTH),
                  _const_spec((1, GLA_WIDTH)), st_spec,
                  _const_spec((GLA_TILE, GLA_TILE)), _const_spec((GLA_TILE, GLA_TILE)),
                  _const_spec((GLA_SUB, GLA_PAIR_ROWS)), _const_spec((GLA_QK, GLA_WIDTH)),
                  _const_spec((GLA_WIDTH, GLA_WIDTH)), _const_spec((GLA_WIDTH, GLA_QK)),
                  _const_spec((GLA_HEADS * half_tile, GLA_QK)), _const_spec((GLA_HEADS * half_tile, GLA_WIDTH))]
                 + [_const_spec((half_tile, GLA_HEADS * half_tile))] * (len(GLA_LEVELS) - 1),
        out_specs=[tok(GLA_WIDTH), st_spec],
        out_shape=[jax.ShapeDtypeStruct((nb, seq, GLA_WIDTH), BF16),
                   jax.ShapeDtypeStruct((nb, 2, GLA_WIDTH, GLA_QK), F32)],
        scratch_shapes=[pltpu.VMEM((seq, GLA_WIDTH), F32), pltpu.VMEM((seq, GLA_WIDTH), F32),
                        pltpu.VMEM((GLA_WIDTH, GLA_QK), F32), pltpu.VMEM((GLA_WIDTH, GLA_QK), F32)],
        compiler_params=_params(("parallel",)),
        name="gla",
    )(qk, gv, la, sg, gnorm, s0t, consts["tril"], consts["triu"], consts["sel"],
      consts["e2"], consts["e3"], consts["mask"], consts["kmask"], consts["vmask"], *consts["lmasks"])


def _tail_kernel(x_ref, f_ref, at_ref, og_ref, gate_ref, mod_ref, wof_ref, wom_ref, wog_ref, wout_ref,
                 g2_ref, w1_ref, w2_ref, gf_ref, o_ref, *, final, ff_chunk):
    mod = mod_ref[0]
    gate1, shift2, scale2, gate2 = mod[2:3], mod[3:4], mod[4:5], mod[5:6]
    ya = _dot(f_ref[0], wof_ref[...])
    yb = _dot(at_ref[0], wom_ref[...])
    yc = _dot(og_ref[0], wog_ref[...])
    merged = (gate_ref[0, :, 0:D_MODEL].astype(F32) * ya
              + gate_ref[0, :, D_MODEL:2 * D_MODEL].astype(F32) * yb
              + gate_ref[0, :, 2 * D_MODEL:3 * D_MODEL].astype(F32) * yc)
    x1 = x_ref[0] + gate1 * _dot(merged.astype(BF16), wout_ref[...])
    h2 = (_rms(x1) * g2_ref[...] * (1.0 + scale2) + shift2).astype(BF16)
    acc = None
    for c0 in range(0, D_FF, ff_chunk):
        u = jnp.maximum(_dot(h2, w1_ref[:, c0:c0 + ff_chunk]), 0.0)
        part = _dot((u * u).astype(BF16), w2_ref[c0:c0 + ff_chunk, :])
        acc = part if acc is None else acc + part
    x2 = x1 + gate2 * acc
    if final:
        x2 = _rms(x2) * gf_ref[...]
    o_ref[0] = x2


def _tail_call(x, f, attn, og, gates, mod, lw, final_g, *, final, tm):
    nb, seq, _ = x.shape
    mod_idx = (lambda b, t: (b, 0, 0)) if mod.shape[0] == nb else (lambda b, t: (0, 0, 0))
    tok = lambda w: pl.BlockSpec((1, tm, w), lambda b, t: (b, t, 0))
    return pl.pallas_call(
        functools.partial(_tail_kernel, final=final, ff_chunk=1024),
        grid=(nb, seq // tm),
        in_specs=[tok(D_MODEL), tok(F_WIDTH), tok(MLA_WIDTH), tok(GLA_WIDTH), tok(N_BRANCH * D_MODEL),
                  pl.BlockSpec((1, N_MOD, D_MODEL), mod_idx),
                  _const_spec((F_WIDTH, D_MODEL)), _const_spec((MLA_WIDTH, D_MODEL)),
                  _const_spec((GLA_WIDTH, D_MODEL)), _const_spec((D_MODEL, D_MODEL)),
                  _const_spec((1, D_MODEL)), _const_spec((D_MODEL, D_FF)), _const_spec((D_FF, D_MODEL)),
                  _const_spec((1, D_MODEL))],
        out_specs=tok(D_MODEL),
        out_shape=jax.ShapeDtypeStruct((nb, seq, D_MODEL), F32),
        compiler_params=_params(("parallel", "parallel")),
        name="tail",
    )(x, f, attn, og, gates, mod, lw["wof"], lw["wom"], lw["wog"], lw["wout"], lw["g2"], lw["wff1"],
      lw["wff2"], final_g)


def _rot_partner(w):
    q = QK_ROPE // 4
    return jnp.concatenate([-w[..., q:2 * q], w[..., 0:q], -w[..., 3 * q:4 * q], w[..., 2 * q:3 * q]], axis=-1)


def _rope_tables(seq):
    half = QK_ROPE // 2
    pos = jnp.arange(seq)
    row = (pos // GRID_W).astype(F32)
    col = (pos % GRID_W).astype(F32)
    inv = ROPE_BASE ** (-jnp.arange(0, half, 2, dtype=F32) / half)
    ang = jnp.concatenate([row[:, None] * inv, row[:, None] * inv, col[:, None] * inv, col[:, None] * inv], axis=1)
    cos, sin = jnp.cos(ang), jnp.sin(ang)
    pad = HEAD_PAD - QK_NOPE - QK_ROPE
    cos_q = jnp.concatenate([jnp.ones((seq, QK_NOPE), F32), cos, jnp.zeros((seq, pad), F32)], axis=1)
    sin_q = jnp.concatenate([jnp.zeros((seq, QK_NOPE), F32), sin, jnp.zeros((seq, pad), F32)], axis=1)
    return dict(cos_q=cos_q, sin_q=sin_q, cos_t=cos_q.T, sin_t=sin_q.T)


def _dft_tables(seq):
    split = math.gcd(seq, GRID_W)
    pos = jnp.arange(seq, dtype=jnp.int32)
    w = 2.0 * math.pi / seq

    def trig(k, axis):
        ang = ((k[:, None] * pos[None, :]) % seq).astype(F32) * w
        return jnp.expand_dims(jnp.cos(ang), axis), jnp.expand_dims(jnp.sin(ang), axis)

    ca, sa = trig(jnp.arange(seq // split, dtype=jnp.int32) * split, 1)
    cb, sb = trig(jnp.arange(split, dtype=jnp.int32), 0)
    cos = (ca * cb - sa * sb).reshape(seq, seq)
    neg_sin = (-(sa * cb + ca * sb)).reshape(seq, seq)
    return cos.astype(BF16), neg_sin.astype(BF16)


def _dft64():
    k = jnp.arange(F_GROUP_DIM, dtype=jnp.int32)
    ang = ((k[:, None] * k[None, :]) % F_GROUP_DIM).astype(F32) * (2.0 * math.pi / F_GROUP_DIM)
    eye = jnp.eye(F_GROUPS, dtype=F32)
    return jnp.concatenate([jnp.kron(eye, jnp.cos(ang)), jnp.kron(eye, jnp.sin(ang))], axis=1).astype(BF16)


def _gla_consts():
    idx = np.arange(GLA_TILE)
    tril = idx[None, :] <= idx[:, None]
    triu = idx[None, :] >= idx[:, None]
    hq = np.arange(GLA_QK) // GLA_DK
    hv = np.arange(GLA_WIDTH) // GLA_DV
    e2 = hq[:, None] == hv[None, :]
    e3 = (hv[:, None] == hv[None, :]) / GLA_DV
    mask = hv[:, None] == hq[None, :]
    sel = np.arange(GLA_SUB)[:, None] == (np.arange(GLA_PAIR_ROWS) // GLA_DIAG)[None, :]
    half_tile = GLA_TILE // 2
    row_head = np.arange(GLA_HEADS * half_tile) // half_tile
    kmask = row_head[:, None] == hq[None, :]
    vmask = row_head[:, None] == hv[None, :]
    pos = np.arange(half_tile)
    key_pos = np.tile(pos, GLA_HEADS)
    lmasks = [jnp.asarray((pos[:, None] // hl) == (key_pos[None, :] // hl), F32) for hl in GLA_LEVELS[1:]]
    return dict(tril=jnp.asarray(tril, BF16), triu=jnp.asarray(triu, BF16),
                sel=jnp.asarray(sel, BF16),
                e2=jnp.asarray(e2, BF16), e3=jnp.asarray(e3, BF16), mask=jnp.asarray(mask, F32),
                kmask=jnp.asarray(kmask, F32), vmask=jnp.asarray(vmask, F32), lmasks=lmasks)


def _layer_weights(l, w_in, norm1_g, mla_q_norm_g, mla_kv_norm_g, w_q_up, w_kv_up, gla_wa2_f, gla_ba_f,
                   gla_wa2_b, gla_ba_b, gla_norm_g, w_o_fourier, w_o_mla, w_o_gla, w_out, norm2_g, w_ff1,
                   w_ff2, dft64):
    wi = w_in[l]
    o = np.cumsum([0, F_WIDTH, Q_LORA, KV_LORA, QK_ROPE, GLA_QK, GLA_QK, GLA_WIDTH, GLA_WIDTH,
                   GLA_GATE_RANK, GLA_GATE_RANK, N_BRANCH * D_MODEL])
    zf, zqd, zkvd, zkr, zgq, zgk, zgv, zgr, zaf, zab, zgate = (wi[:, int(o[i]):int(o[i + 1])] for i in range(11))
    small = jnp.concatenate([zkr, jnp.zeros((D_MODEL, S_AF - QK_ROPE), F32), zaf, zab,
                             jnp.zeros((D_MODEL, LANES - S_AB - GLA_GATE_RANK), F32)], axis=1)
    w1 = jnp.concatenate([zf, zqd, zkvd, zgq, zgk, small, zgv, zgr, zgate], axis=1).astype(BF16)
    lane_pad = lambda w: jnp.pad(w, ((0, 0), (QK_NOPE, HEAD_PAD - QK_NOPE - QK_ROPE)))
    wkr = jnp.concatenate([lane_pad(zkr), lane_pad(_rot_partner(zkr))], axis=1).astype(BF16)

    wq3 = w_q_up[l].reshape(Q_LORA, MLA_HEADS, QK_NOPE + QK_ROPE)
    pad = jnp.zeros((Q_LORA, MLA_HEADS, HEAD_PAD - QK_NOPE - QK_ROPE), F32)
    wqt = jnp.concatenate([wq3, pad], axis=2).reshape(Q_LORA, -1).T.astype(BF16)
    wqrt = jnp.concatenate([jnp.zeros((Q_LORA, MLA_HEADS, QK_NOPE), F32), _rot_partner(wq3[..., QK_NOPE:]), pad],
                           axis=2).reshape(Q_LORA, -1).T.astype(BF16)
    wkv3 = w_kv_up[l].reshape(KV_LORA, MLA_HEADS, QK_NOPE + V_DIM)
    wk = jnp.pad(wkv3[..., :QK_NOPE], ((0, 0), (0, 0), (0, HEAD_PAD - QK_NOPE))).reshape(KV_LORA, -1).astype(BF16)
    wvt = wkv3[..., QK_NOPE:].reshape(KV_LORA, -1).T.astype(BF16)

    wa = jnp.zeros((LANES, 2 * GLA_QK), F32)
    wa = wa.at[S_AF:S_AF + GLA_GATE_RANK, 0:GLA_QK].set(gla_wa2_f[l])
    wa = wa.at[S_AB:S_AB + GLA_GATE_RANK, GLA_QK:].set(gla_wa2_b[l]).astype(BF16)
    ba = jnp.concatenate([gla_ba_f[l], gla_ba_b[l]])[None]
    return dict(
        g1=norm1_g[l][None], w1=w1, dft64=dft64, gq=mla_q_norm_g[l][None], wqt=wqt, wqrt=wqrt,
        gkv=mla_kv_norm_g[l][None], wk=wk, wvt=wvt, wkr=wkr, wa=wa, ba=ba, gn=gla_norm_g[l][None],
        wof=w_o_fourier[l].astype(BF16), wom=w_o_mla[l].astype(BF16), wog=w_o_gla[l].astype(BF16),
        wout=w_out[l].astype(BF16), g2=norm2_g[l][None], wff1=w_ff1[l].astype(BF16),
        wff2=w_ff2[l].astype(BF16))


def _state_to_blockdiag_t(s):
    nb = s.shape[0]
    eye = jnp.eye(GLA_HEADS, dtype=s.dtype)
    st = jnp.einsum("bxhde,hg->bxhegd", s, eye)
    return st.reshape(nb, 2, GLA_WIDTH, GLA_QK)


def _blockdiag_t_to_state(st):
    nb = st.shape[0]
    s6 = st.reshape(nb, 2, GLA_HEADS, GLA_DV, GLA_HEADS, GLA_DK)
    diag = jnp.stack([s6[:, :, h, :, h, :] for h in range(GLA_HEADS)], axis=2)
    return jnp.swapaxes(diag, -1, -2)


def _pick_tile(seq, target):
    t = min(seq, target)
    assert seq % t == 0 and t % GLA_SUB == 0, (seq, t)
    return t


def _mixer_and_tail(x, mod, lw, tabs, dft, gla_c, cache, s0t, final_g, *, rope, final):
    nb, seq, _ = x.shape
    tm = _pick_tile(seq, 512)
    outs = _inproj_call(x, mod, lw, tabs, rope=rope, emit_ctx=not rope, tm=tm)
    fcs, qt, k, vt, qk, gv, la, sg, gates = outs[:9]
    f = _fourier_call(fcs, dft[0], dft[1], tr=_pick_tile(seq, 512))
    tq = _pick_tile(seq, 1024)
    hp = MLA_HEADS if seq <= 512 else 2
    attn = _attn_t_call(qt, k, vt, cache, tq=tq, ck=512, ru=min(tq, 256), hp=hp)
    og, st_out = _gla_call(qk, gv, la, sg, lw["gn"], s0t, gla_c)
    y = _tail_call(x, f, attn, og, gates, mod, lw, final_g, final=final, tm=tm)
    return y, outs[9:], st_out


def kernel(x_prompt, x_sample, c, cache_mla_ckv, cache_mla_krope, state_gla, c_ctx, w_mod, b_mod, norm1_g, w_in, mla_q_norm_g, mla_kv_norm_g, w_q_up, w_kv_up, gla_wa2_f, gla_ba_f, gla_wa2_b, gla_ba_b, gla_norm_g, w_o_fourier, w_o_mla, w_o_gla, w_out, norm2_g, w_ff1, w_ff2, final_norm_g):
    depth = w_mod.shape[0]
    n_ctx, seq_ctx, _ = x_prompt.shape
    n_lat, seq_lat, _ = x_sample.shape
    assert seq_ctx % GLA_TILE == 0 and seq_lat % GLA_TILE == 0 and seq_lat % GRID_W == 0
    assert 1 + n_lat <= MOD_ROWS

    cond = jnp.concatenate([c_ctx[None], c, jnp.zeros((MOD_ROWS - 1 - n_lat, D_MODEL), F32)], axis=0)
    mod_all = _mod_call(cond, w_mod, b_mod).reshape(depth, MOD_ROWS, N_MOD, D_MODEL)

    dft64 = _dft64()
    gla_c = _gla_consts()
    tabs = _rope_tables(seq_lat)
    dft_ctx = _dft_tables(seq_ctx)
    dft_lat = _dft_tables(seq_lat)
    final_g = final_norm_g[None]
    zero_state = jnp.zeros((n_ctx, 2, GLA_WIDTH, GLA_QK), F32)

    xp, xs = x_prompt, x_sample
    ckv_list, kr_list, gla_list = [], [], []
    for l in range(depth):
        lw = _layer_weights(l, w_in, norm1_g, mla_q_norm_g, mla_kv_norm_g, w_q_up, w_kv_up, gla_wa2_f,
                            gla_ba_f, gla_wa2_b, gla_ba_b, gla_norm_g, w_o_fourier, w_o_mla, w_o_gla,
                            w_out, norm2_g, w_ff1, w_ff2, dft64)
        final = l == depth - 1
        mod_ctx = mod_all[l, 0:1]
        mod_lat = mod_all[l, 1:1 + n_lat]

        xp, (ckv_c, kr_c), st_c = _mixer_and_tail(xp, mod_ctx, lw, None, dft_ctx, gla_c, None, zero_state,
                                                  final_g, rope=False, final=final)
        ckv_list.append(ckv_c)
        kr_list.append(kr_c)
        gla_list.append(_blockdiag_t_to_state(st_c))

        kr_padded = jnp.pad(cache_mla_krope[:, l], ((0, 0), (0, 0), (QK_NOPE, HEAD_PAD - QK_NOPE - QK_ROPE)))
        cache = _kvcache_call(cache_mla_ckv[:, l], kr_padded, lw)
        s0t = _state_to_blockdiag_t(state_gla[:, l].astype(F32))
        xs, _, _ = _mixer_and_tail(xs, mod_lat, lw, tabs, dft_lat, gla_c, cache, s0t, final_g,
                                   rope=True, final=final)

    return (xp, xs, jnp.stack(ckv_list, axis=1), jnp.stack(kr_list, axis=1), jnp.stack(gla_list, axis=1))
```
